```python
import jax, jax.numpy as jnp
from jax import lax
import numpy as np

D_MODEL = 1024
BATCH = 1
SEQ = 16384
DEPTH = 4
DEC_BATCH = 32
DEC_SEQ = 2048
PAST_LEN = 128

HEAD_DIM = 64
MIX_WIDTH = D_MODEL
N_ATTN_HEADS = (MIX_WIDTH // 2) // HEAD_DIM
N_KV_HEADS = 2
N_RET_HEADS = (MIX_WIDTH // 2) // HEAD_DIM
ATTN_WIDTH = N_ATTN_HEADS * HEAD_DIM
KV_WIDTH = N_KV_HEADS * HEAD_DIM
RET_WIDTH = N_RET_HEADS * HEAD_DIM
IN_COLS = ATTN_WIDTH + 2 * KV_WIDTH + 4 * RET_WIDTH
WINDOW = 128
BLOCK = 128
RET_CHUNK = 128
ROPE_THETA = 500000.0
ROPE_DIMS = HEAD_DIM // 4
N_EXPERTS = 16
CAPACITY_FACTOR = 2
D_FF = 2 * D_MODEL
EPS = 1e-6
NEG = -1e30

kernel_name = 'hymba_style_window_attn_retention_ec_moe_encoder'


def rmsnorm(x, g):
    xf = x.astype(jnp.float32)
    y = xf * lax.rsqrt(jnp.mean(xf * xf, axis=-1, keepdims=True) + EPS)
    return (y * g.astype(jnp.float32)).astype(x.dtype)


def rope_partial(x, pos):
    half = ROPE_DIMS // 2
    inv_freq = jnp.exp(-jnp.log(jnp.float32(ROPE_THETA)) * jnp.arange(half, dtype=jnp.float32) * (2.0 / ROPE_DIMS))
    ang = pos[:, None] * inv_freq[None, :]
    cos = jnp.cos(ang)[None, :, None, :]
    sin = jnp.sin(ang)[None, :, None, :]
    xf = x.astype(jnp.float32)
    x1 = xf[..., :half]
    x2 = xf[..., half:ROPE_DIMS]
    out = jnp.concatenate([x1 * cos - x2 * sin, x2 * cos + x1 * sin, xf[..., ROPE_DIMS:]], axis=-1)
    return out.astype(x.dtype)


def windowed_gqa(q, k, v, sink):
    B, S = q.shape[0], q.shape[1]
    nb = S // BLOCK
    grp = N_ATTN_HEADS // N_KV_HEADS
    qb = q.reshape(B, nb, BLOCK, N_KV_HEADS, grp, HEAD_DIM)

    def band(t):
        tp = jnp.pad(t, ((0, 0), (BLOCK, BLOCK), (0, 0), (0, 0))).reshape(B, nb + 2, BLOCK, N_KV_HEADS, HEAD_DIM)
        return jnp.concatenate([tp[:, :-2], tp[:, 1:-1], tp[:, 2:]], axis=2)

    kb, vb = band(k), band(v)
    scores = jnp.einsum('bnqhgd,bnkhd->bnhgqk', qb, kb).astype(jnp.float32) * (HEAD_DIM ** -0.5)
    blk = jnp.arange(nb)[:, None, None] * BLOCK
    qpos = blk + jnp.arange(BLOCK)[None, :, None]
    kpos = blk - BLOCK + jnp.arange(3 * BLOCK)[None, None, :]
    valid = (jnp.abs(qpos - kpos) <= WINDOW) & (kpos >= 0) & (kpos < S)
    scores = jnp.where(valid[None, :, None, None], scores, NEG)
    sink_l = sink.astype(jnp.float32).reshape(N_KV_HEADS, grp)[None, None, :, :, None, None]
    m = jnp.maximum(jnp.max(scores, axis=-1, keepdims=True), sink_l)
    p = jnp.exp(scores - m)
    denom = jnp.sum(p, axis=-1, keepdims=True) + jnp.exp(sink_l - m)
    probs = (p / denom).astype(v.dtype)
    out = jnp.einsum('bnhgqk,bnkhd->bnqhgd', probs, vb)
    return out.reshape(B, S, ATTN_WIDTH)


def retention_dir(q, k, v, log_gamma, strict):
    B, H = q.shape[0], q.shape[1]
    C = RET_CHUNK
    idx = jnp.arange(C, dtype=jnp.float32)
    diff = idx[:, None] - idx[None, :]
    mask = (diff > 0) if strict else (diff >= 0)
    decay = jnp.where(mask[None], jnp.exp(log_gamma[:, None, None] * jnp.where(mask, diff, 0.0)[None]), 0.0)
    inner = jnp.einsum('bhnqd,bhnkd->bhnqk', q, k) * decay[None, :, None]
    inner_out = jnp.einsum('bhnqk,bhnkd->bhnqd', inner, v)
    zeta = jnp.exp(log_gamma[:, None] * (C - 1 - idx)[None, :])
    chunk_kv = jnp.einsum('bhnkd,bhnke->bhnde', k * zeta[None, :, None, :, None], v)
    chunk_decay = jnp.exp(log_gamma * C)[None, :, None, None]

    def step(state, kv):
        return chunk_decay * state + kv, state

    init = jnp.zeros((B, H, HEAD_DIM, HEAD_DIM), jnp.float32)
    _, prev = lax.scan(step, init, jnp.moveaxis(chunk_kv, 2, 0))
    prev = jnp.moveaxis(prev, 0, 2)
    xi = jnp.exp(log_gamma[:, None] * (idx + 1.0)[None, :])
    cross = jnp.einsum('bhnqd,bhnde->bhnqe', q * xi[None, :, None, :, None], prev)
    return inner_out + cross


def bidirectional_retention(q, k, v, gate, decay_f, decay_b, gn_g):
    B, S = q.shape[0], q.shape[1]
    nc = S // RET_CHUNK

    def heads(t):
        return t.astype(jnp.float32).reshape(B, nc, RET_CHUNK, N_RET_HEADS, HEAD_DIM).transpose(0, 3, 1, 2, 4)

    qh, kh, vh = heads(q), heads(k) * (HEAD_DIM ** -0.5), heads(v)
    flip = lambda t: t[:, :, ::-1, ::-1]
    fwd = retention_dir(qh, kh, vh, jax.nn.log_sigmoid(decay_f.astype(jnp.float32)), False)
    bwd = flip(retention_dir(flip(qh), flip(kh), flip(vh), jax.nn.log_sigmoid(decay_b.astype(jnp.float32)), True))
    o = fwd + bwd
    mu = jnp.mean(o, axis=-1, keepdims=True)
    var = jnp.mean(jnp.square(o - mu), axis=-1, keepdims=True)
    o = (o - mu) * lax.rsqrt(var + EPS)
    o = o.transpose(0, 2, 3, 1, 4).reshape(B, S, RET_WIDTH) * gn_g.astype(jnp.float32)
    return (jax.nn.silu(gate.astype(jnp.float32)) * o).astype(q.dtype)


def expert_choice_ffn(h, w_router, w_gate, w_up, w_down):
    B, S, D = h.shape
    T = B * S
    cap = CAPACITY_FACTOR * T // N_EXPERTS
    xt = h.reshape(T, D)
    affinity = jax.nn.softmax(jnp.matmul(xt, w_router).astype(jnp.float32), axis=-1)
    gates, idx = lax.top_k(affinity.T, cap)
    xe = xt[idx]
    hid = jax.nn.silu(jnp.einsum('ecd,edf->ecf', xe, w_gate)) * jnp.einsum('ecd,edf->ecf', xe, w_up)
    ye = jnp.einsum('ecf,efd->ecd', hid, w_down) * gates[..., None].astype(h.dtype)
    out = jnp.zeros((T, D), h.dtype).at[idx.reshape(-1)].add(ye.reshape(-1, D))
    return out.reshape(B, S, D)


def trunk(x, norm1_g, w_in, attn_sink, ret_decay_fwd, ret_decay_bwd, attn_out_g, ret_out_g,
          w_out, norm2_g, w_router, w_gate, w_up, w_down, final_norm_g):
    B, S, _ = x.shape
    pos = jnp.arange(S, dtype=jnp.float32)
    cuts = [ATTN_WIDTH, ATTN_WIDTH + KV_WIDTH, ATTN_WIDTH + 2 * KV_WIDTH,
            ATTN_WIDTH + 2 * KV_WIDTH + RET_WIDTH, ATTN_WIDTH + 2 * KV_WIDTH + 2 * RET_WIDTH,
            ATTN_WIDTH + 2 * KV_WIDTH + 3 * RET_WIDTH]
    for l in range(DEPTH):
        h = rmsnorm(x, norm1_g[l])
        proj = jnp.matmul(h, w_in[l])
        aq, ak, av, rq, rk, rv, rg = jnp.split(proj, cuts, axis=-1)
        aq = rope_partial(aq.reshape(B, S, N_ATTN_HEADS, HEAD_DIM), pos)
        ak = rope_partial(ak.reshape(B, S, N_KV_HEADS, HEAD_DIM), pos)
        av = av.reshape(B, S, N_KV_HEADS, HEAD_DIM)
        a_out = rmsnorm(windowed_gqa(aq, ak, av, attn_sink[l]), attn_out_g[l])
        r_out = bidirectional_retention(rq, rk, rv, rg, ret_decay_fwd[l], ret_decay_bwd[l], ret_out_g[l])
        mixed = jnp.concatenate([a_out, r_out], axis=-1)
        x = x + jnp.matmul(mixed, w_out[l])
        h2 = rmsnorm(x, norm2_g[l])
        x = x + expert_choice_ffn(h2, w_router[l], w_gate[l], w_up[l], w_down[l])
    return rmsnorm(x, final_norm_g)


def setup_inputs(seed: int = 0) -> dict:
    key = jax.random.key(seed)
    ks = jax.random.split(key, 20)
    f32 = jnp.float32
    nrm = lambda k, shape, s: jax.random.normal(k, shape, f32) * s
    e = 5.0 + np.arange(N_RET_HEADS, dtype=np.float32)
    retnet_logit = jnp.asarray(np.log((1.0 - 2.0 ** (-e)) / (2.0 ** (-e))).astype(np.float32))
    return {
        'x_prompt': jax.random.normal(ks[0], (BATCH, SEQ, D_MODEL), f32),
        'x_sample': jax.random.normal(ks[1], (DEC_BATCH, DEC_SEQ, D_MODEL), f32),
        'norm1_g': 1.0 + nrm(ks[2], (DEPTH, D_MODEL), 0.02),
        'w_in': nrm(ks[3], (DEPTH, D_MODEL, IN_COLS), D_MODEL ** -0.5),
        'attn_sink': nrm(ks[4], (DEPTH, N_ATTN_HEADS), 0.5),
        'ret_decay_fwd': retnet_logit[None, :] + nrm(ks[5], (DEPTH, N_RET_HEADS), 0.1),
        'ret_decay_bwd': retnet_logit[None, :] + nrm(ks[6], (DEPTH, N_RET_HEADS), 0.1),
        'attn_out_g': 1.0 + nrm(ks[7], (DEPTH, ATTN_WIDTH), 0.02),
        'ret_out_g': 1.0 + nrm(ks[8], (DEPTH, RET_WIDTH), 0.02),
        'w_out': nrm(ks[9], (DEPTH, MIX_WIDTH, D_MODEL), MIX_WIDTH ** -0.5),
        'norm2_g': 1.0 + nrm(ks[10], (DEPTH, D_MODEL), 0.02),
        'w_router': nrm(ks[11], (DEPTH, D_MODEL, N_EXPERTS), D_MODEL ** -0.5),
        'w_gate': nrm(ks[12], (DEPTH, N_EXPERTS, D_MODEL, D_FF), D_MODEL ** -0.5),
        'w_up': nrm(ks[13], (DEPTH, N_EXPERTS, D_MODEL, D_FF), D_MODEL ** -0.5),
        'w_down': nrm(ks[14], (DEPTH, N_EXPERTS, D_FF, D_MODEL), D_FF ** -0.5),
        'final_norm_g': 1.0 + nrm(ks[15], (D_MODEL,), 0.02),
    }


def reference(x_prompt, x_sample, norm1_g, w_in, attn_sink, ret_decay_fwd, ret_decay_bwd, attn_out_g,
              ret_out_g, w_out, norm2_g, w_router, w_gate, w_up, w_down, final_norm_g):
    y_prompt = trunk(x_prompt, norm1_g, w_in, attn_sink, ret_decay_fwd, ret_decay_bwd, attn_out_g, ret_out_g,
                     w_out, norm2_g, w_router, w_gate, w_up, w_down, final_norm_g)
    y_sample = trunk(x_sample, norm1_g, w_in, attn_sink, ret_decay_fwd, ret_decay_bwd, attn_out_g, ret_out_g,
                     w_out, norm2_g, w_router, w_gate, w_up, w_down, final_norm_g)
    return (y_prompt, y_sample)
```

```python
import functools

import jax
import jax.numpy as jnp
from jax import lax
from jax.experimental import pallas as pl
from jax.experimental.pallas import tpu as pltpu

f32 = jnp.float32
bf16 = jnp.bfloat16
i32 = jnp.int32

D_MODEL = 1024
HEAD_DIM = 64
N_ATTN_HEADS = 8
N_KV_HEADS = 2
N_RET_HEADS = 8
ATTN_WIDTH = N_ATTN_HEADS * HEAD_DIM
KV_WIDTH = N_KV_HEADS * HEAD_DIM
RET_WIDTH = N_RET_HEADS * HEAD_DIM
QKV_COLS = ATTN_WIDTH + 2 * KV_WIDTH
IN_COLS = QKV_COLS + 4 * RET_WIDTH
WINDOW = 128
BLOCK = 128
RET_CHUNK = 128
ROPE_THETA = 500000.0
ROPE_DIMS = HEAD_DIM // 4
N_EXPERTS = 16
CAPACITY_FACTOR = 2
D_FF = 2 * D_MODEL
EPS = 1e-6
NEG = -1e30

LANES = 128
BF16_ROWS = 16
GATE_COLS = LANES
XAUG_COLS = D_MODEL + GATE_COLS
TOKEN_TILE = 256
DISPATCH_ROWS = 64
COMBINE_ROWS = 128
PROJ_TILE = 512
RET_CHUNKS_PER_STEP = 4
VMEM_LIMIT = 48 * 1024 * 1024


def _cparams(sem):
    return pltpu.CompilerParams(dimension_semantics=sem, vmem_limit_bytes=VMEM_LIMIT)


def _sigmoid(x):
    return 1.0 / (1.0 + jnp.exp(-x))


def _log_sigmoid(x):
    return jnp.minimum(x, 0.0) - jnp.log(1.0 + jnp.exp(-jnp.abs(x)))


def _inproj_kernel(x_ref, g_ref, w_ref, c_ref, s1_ref, s2_ref, q_ref, kv_ref, ret_ref):
    x = x_ref[...]
    h = x * lax.rsqrt(jnp.mean(x * x, axis=-1, keepdims=True) + EPS) * g_ref[...]
    hb = h.astype(bf16)
    c, s1, s2 = c_ref[...], s1_ref[...], s2_ref[...]

    def rope(p):
        half = ROPE_DIMS // 2
        return p * c + pltpu.roll(p, half, 1) * s1 + pltpu.roll(p, LANES - half, 1) * s2

    qkv = jnp.dot(hb, w_ref[:, 0:QKV_COLS], preferred_element_type=f32)
    for j in range(ATTN_WIDTH // LANES):
        q_ref[:, j * LANES:(j + 1) * LANES] = rope(qkv[:, j * LANES:(j + 1) * LANES]).astype(bf16)
    kv_ref[:, 0:KV_WIDTH] = rope(qkv[:, ATTN_WIDTH:ATTN_WIDTH + KV_WIDTH]).astype(bf16)
    kv_ref[:, KV_WIDTH:2 * KV_WIDTH] = qkv[:, ATTN_WIDTH + KV_WIDTH:QKV_COLS].astype(bf16)
    for j in range(4):
        lo = QKV_COLS + j * RET_WIDTH
        ret_ref[:, j * RET_WIDTH:(j + 1) * RET_WIDTH] = jnp.dot(
            hb, w_ref[:, lo:lo + RET_WIDTH], preferred_element_type=f32).astype(bf16)


def _inproj(x, g, w, rc, rs1, rs2, seq):
    T = x.shape[0]
    tm = PROJ_TILE
    npos = seq // tm
    row = lambda i: (i, 0)
    pos = lambda i: (i % npos, 0)
    fixed = lambda i: (0, 0)
    return pl.pallas_call(
        _inproj_kernel,
        grid=(T // tm,),
        in_specs=[pl.BlockSpec((tm, D_MODEL), row), pl.BlockSpec((1, D_MODEL), fixed),
                  pl.BlockSpec((D_MODEL, IN_COLS), fixed),
                  pl.BlockSpec((tm, LANES), pos), pl.BlockSpec((tm, LANES), pos),
                  pl.BlockSpec((tm, LANES), pos)],
        out_specs=[pl.BlockSpec((tm, ATTN_WIDTH), row), pl.BlockSpec((tm, 2 * KV_WIDTH), row),
                   pl.BlockSpec((tm, 4 * RET_WIDTH), row)],
        out_shape=[jax.ShapeDtypeStruct((T, ATTN_WIDTH), bf16),
                   jax.ShapeDtypeStruct((T, 2 * KV_WIDTH), bf16),
                   jax.ShapeDtypeStruct((T, 4 * RET_WIDTH), bf16)],
        compiler_params=_cparams(("arbitrary",)),
        name="inproj",
    )(x, g, w, rc, rs1, rs2)


def _attn_kernel(q_ref, kp_ref, kc_ref, kn_ref, sink_ref, g_ref, o_ref):
    n = pl.program_id(1)
    nb = pl.num_programs(1)
    kv = jnp.concatenate([kp_ref[...], kc_ref[...], kn_ref[...]], axis=0)
    qi = lax.broadcasted_iota(i32, (BLOCK, 3 * BLOCK), 0)
    ki = lax.broadcasted_iota(i32, (BLOCK, 3 * BLOCK), 1)
    rel = ki - BLOCK - qi
    valid = (jnp.abs(rel) <= WINDOW) & ((ki >= BLOCK) | (n > 0)) & ((ki < 2 * BLOCK) | (n < nb - 1))
    grp = N_ATTN_HEADS // N_KV_HEADS
    outs = []
    for h in range(N_ATTN_HEADS):
        g = h // grp
        qh = q_ref[:, h * HEAD_DIM:(h + 1) * HEAD_DIM]
        kh = kv[:, g * HEAD_DIM:(g + 1) * HEAD_DIM]
        vh = kv[:, KV_WIDTH + g * HEAD_DIM:KV_WIDTH + (g + 1) * HEAD_DIM]
        s = lax.dot_general(qh, kh, (((1,), (1,)), ((), ())), preferred_element_type=f32)
        s = jnp.where(valid, s * (HEAD_DIM ** -0.5), NEG)
        sk = sink_ref[h:h + 1, 0:1]
        m = jnp.maximum(jnp.max(s, axis=-1, keepdims=True), sk)
        p = jnp.exp(s - m)
        denom = jnp.sum(p, axis=-1, keepdims=True) + jnp.exp(sk - m)
        o = jnp.dot(p.astype(bf16), vh, preferred_element_type=f32)
        outs.append(o / denom)
    a = jnp.concatenate(outs, axis=1)
    a = a * lax.rsqrt(jnp.mean(a * a, axis=-1, keepdims=True) + EPS) * g_ref[...]
    o_ref[...] = a.astype(bf16)


def _attention(q, kv, sink_rows, g, batch, seq):
    T = q.shape[0]
    nb = seq // BLOCK
    cur = lambda b, n: (b * nb + n, 0)
    prev = lambda b, n: (b * nb + jnp.maximum(n - 1, 0), 0)
    nxt = lambda b, n: (b * nb + jnp.minimum(n + 1, nb - 1), 0)
    fixed = lambda b, n: (0, 0)
    return pl.pallas_call(
        _attn_kernel,
        grid=(batch, nb),
        in_specs=[pl.BlockSpec((BLOCK, ATTN_WIDTH), cur),
                  pl.BlockSpec((BLOCK, 2 * KV_WIDTH), prev),
                  pl.BlockSpec((BLOCK, 2 * KV_WIDTH), cur),
                  pl.BlockSpec((BLOCK, 2 * KV_WIDTH), nxt),
                  pl.BlockSpec((N_ATTN_HEADS, LANES), fixed),
                  pl.BlockSpec((1, ATTN_WIDTH), fixed)],
        out_specs=pl.BlockSpec((BLOCK, ATTN_WIDTH), cur),
        out_shape=jax.ShapeDtypeStruct((T, ATTN_WIDTH), bf16),
        compiler_params=_cparams(("arbitrary", "arbitrary")),
        name="attention",
    )(q, kv, kv, kv, sink_rows, g)


N_PAIRS = RET_WIDTH // LANES


def _pair_masks():
    r = lax.broadcasted_iota(i32, (LANES, LANES), 0)
    c = lax.broadcasted_iota(i32, (LANES, LANES), 1)
    low = c < HEAD_DIM
    blockdiag = (r < HEAD_DIM) == (c < HEAD_DIM)
    return low, blockdiag


def _ret_bstate_kernel(k_ref, v_ref, db_ref, sb_ref, st_ref):
    @pl.when(pl.program_id(1) == 0)
    def _():
        st_ref[...] = jnp.zeros(st_ref.shape, f32)

    C = RET_CHUNK
    lg = _log_sigmoid(db_ref[...])
    idx = lax.broadcasted_iota(i32, (C, RET_WIDTH), 0).astype(f32)
    zeta = jnp.exp(lg * idx)
    gc = jnp.exp(lg * float(C))
    _, blockdiag = _pair_masks()
    for c in reversed(range(RET_CHUNKS_PER_STEP)):
        rows = slice(c * C, (c + 1) * C)
        for j in range(N_PAIRS):
            lanes = slice(j * LANES, (j + 1) * LANES)
            kp = k_ref[rows, lanes].astype(f32) * (HEAD_DIM ** -0.5)
            vp = v_ref[rows, lanes]
            st = st_ref[j]
            sb_ref[c, j] = st.astype(bf16)
            kz = (kp * zeta[:, lanes]).T.astype(bf16)
            upd = jnp.dot(kz, vp, preferred_element_type=f32)
            st_ref[j] = jnp.where(blockdiag, gc[:, lanes] * st + upd, 0.0)


def _ret_main_kernel(q_ref, k_ref, v_ref, gate_ref, sb_ref, df_ref, db_ref, gn_ref, o_ref, st_ref):
    @pl.when(pl.program_id(1) == 0)
    def _():
        st_ref[...] = jnp.zeros(st_ref.shape, f32)

    C = RET_CHUNK
    lgf = _log_sigmoid(df_ref[...])
    lgb = _log_sigmoid(db_ref[...])
    idx = lax.broadcasted_iota(i32, (C, RET_WIDTH), 0).astype(f32)
    xi_f = jnp.exp(lgf * (idx + 1.0))
    xi_b = jnp.exp(lgb * (float(C) - idx))
    zeta_f = jnp.exp(lgf * (float(C - 1) - idx))
    gc_f = jnp.exp(lgf * float(C))
    low, blockdiag = _pair_masks()
    diff = (lax.broadcasted_iota(i32, (C, C), 0) - lax.broadcasted_iota(i32, (C, C), 1)).astype(f32)

    def head_decay(h):
        a = lgf[:, h * HEAD_DIM:h * HEAD_DIM + 1]
        b = lgb[:, h * HEAD_DIM:h * HEAD_DIM + 1]
        return jnp.where(diff >= 0.0, jnp.exp(a * jnp.maximum(diff, 0.0)),
                         jnp.exp(b * jnp.maximum(-diff, 0.0)))

    decay = [jnp.concatenate([head_decay(2 * j), head_decay(2 * j + 1)], axis=0) for j in range(N_PAIRS)]
    gn = gn_ref[...]
    inv_hd = 1.0 / HEAD_DIM

    for c in range(RET_CHUNKS_PER_STEP):
        rows = slice(c * C, (c + 1) * C)
        for j in range(N_PAIRS):
            lanes = slice(j * LANES, (j + 1) * LANES)
            qp = q_ref[rows, lanes]
            qf = qp.astype(f32)
            kp = k_ref[rows, lanes].astype(f32) * (HEAD_DIM ** -0.5)
            kb = kp.astype(bf16)
            vp = v_ref[rows, lanes]
            zero = jnp.zeros_like(qp)
            q2 = jnp.concatenate([jnp.where(low, qp, zero), jnp.where(low, zero, qp)], axis=0)
            s2 = lax.dot_general(q2, kb, (((1,), (1,)), ((), ())), preferred_element_type=f32)
            o2 = jnp.dot((s2 * decay[j]).astype(bf16), vp, preferred_element_type=f32)
            inner = jnp.where(low, o2[0:C], o2[C:2 * C])
            st = st_ref[j]
            cross_f = jnp.dot((qf * xi_f[:, lanes]).astype(bf16), st.astype(bf16),
                              preferred_element_type=f32)
            cross_b = jnp.dot((qf * xi_b[:, lanes]).astype(bf16), sb_ref[c, j],
                              preferred_element_type=f32)
            o = inner + cross_f + cross_b
            s_lo = jnp.sum(jnp.where(low, o, 0.0), axis=-1, keepdims=True)
            s_hi = jnp.sum(jnp.where(low, 0.0, o), axis=-1, keepdims=True)
            d = o - jnp.where(low, s_lo, s_hi) * inv_hd
            dd = d * d
            v_lo = jnp.sum(jnp.where(low, dd, 0.0), axis=-1, keepdims=True)
            v_hi = jnp.sum(jnp.where(low, 0.0, dd), axis=-1, keepdims=True)
            var = jnp.where(low, v_lo, v_hi) * inv_hd
            on = d * lax.rsqrt(var + EPS) * gn[:, lanes]
            gate = gate_ref[rows, lanes].astype(f32)
            o_ref[rows, lanes] = (gate * _sigmoid(gate) * on).astype(bf16)
            kz = (kp * zeta_f[:, lanes]).T.astype(bf16)
            upd = jnp.dot(kz, vp, preferred_element_type=f32)
            st_ref[j] = jnp.where(blockdiag, gc_f[:, lanes] * st + upd, 0.0)


def _retention(ret, dec_f, dec_b, gn, batch, seq):
    T = ret.shape[0]
    C = RET_CHUNK
    ch = RET_CHUNKS_PER_STEP
    nblk = seq // (C * ch)
    rows = ch * C
    fixed = lambda b, i: (0, 0)
    col = lambda j: (lambda b, i: (b * nblk + i, j))
    rcol = lambda j: (lambda b, i: (b * nblk + nblk - 1 - i, j))
    sb = pl.pallas_call(
        _ret_bstate_kernel,
        grid=(batch, nblk),
        in_specs=[pl.BlockSpec((rows, RET_WIDTH), rcol(1)), pl.BlockSpec((rows, RET_WIDTH), rcol(2)),
                  pl.BlockSpec((1, RET_WIDTH), fixed)],
        out_specs=pl.BlockSpec((ch, N_PAIRS, LANES, LANES), lambda b, i: (b * nblk + nblk - 1 - i, 0, 0, 0)),
        out_shape=jax.ShapeDtypeStruct((T // C, N_PAIRS, LANES, LANES), bf16),
        scratch_shapes=[pltpu.VMEM((N_PAIRS, LANES, LANES), f32)],
        compiler_params=_cparams(("arbitrary", "arbitrary")),
        name="ret_bstate",
    )(ret, ret, dec_b)
    return pl.pallas_call(
        _ret_main_kernel,
        grid=(batch, nblk),
        in_specs=[pl.BlockSpec((rows, RET_WIDTH), col(0)), pl.BlockSpec((rows, RET_WIDTH), col(1)),
                  pl.BlockSpec((rows, RET_WIDTH), col(2)), pl.BlockSpec((rows, RET_WIDTH), col(3)),
                  pl.BlockSpec((ch, N_PAIRS, LANES, LANES), lambda b, i: (b * nblk + i, 0, 0, 0)),
                  pl.BlockSpec((1, RET_WIDTH), fixed), pl.BlockSpec((1, RET_WIDTH), fixed),
                  pl.BlockSpec((1, RET_WIDTH), fixed)],
        out_specs=pl.BlockSpec((rows, RET_WIDTH), col(0)),
        out_shape=jax.ShapeDtypeStruct((T, RET_WIDTH), bf16),
        scratch_shapes=[pltpu.VMEM((N_PAIRS, LANES, LANES), f32)],
        compiler_params=_cparams(("arbitrary", "arbitrary")),
        name="ret_main",
    )(ret, ret, ret, ret, sb, dec_f, dec_b, gn)


def _outproj_kernel(a_ref, r_ref, x_ref, w_ref, g_ref, wr_ref, x1_ref, xaug_ref, afft_ref):
    mixed = jnp.concatenate([a_ref[...], r_ref[...]], axis=1)
    x1 = x_ref[...] + jnp.dot(mixed, w_ref[...], preferred_element_type=f32)
    x1_ref[...] = x1
    h2 = x1 * lax.rsqrt(jnp.mean(x1 * x1, axis=-1, keepdims=True) + EPS) * g_ref[...]
    hb = h2.astype(bf16)
    xaug_ref[:, 0:D_MODEL] = hb
    logits = jnp.dot(hb, wr_ref[...], preferred_element_type=f32)
    lane = lax.broadcasted_iota(i32, logits.shape, 1)
    real = lane < N_EXPERTS
    lg = jnp.where(real, logits, NEG)
    e = jnp.where(real, jnp.exp(lg - jnp.max(lg, axis=-1, keepdims=True)), 0.0)
    aff = e / jnp.sum(e, axis=-1, keepdims=True)
    afft_ref[...] = aff.T[0:N_EXPERTS, :]
    hi = aff.astype(bf16).astype(f32)
    r1 = aff - hi
    mid = r1.astype(bf16).astype(f32)
    lo = (r1 - mid).astype(bf16).astype(f32)
    cols = hi + pltpu.roll(mid, N_EXPERTS, 1) + pltpu.roll(lo, 2 * N_EXPERTS, 1)
    xaug_ref[:, D_MODEL:XAUG_COLS] = cols.astype(bf16)


def _outproj(a, r, x, w, g, wr):
    T = x.shape[0]
    tm = PROJ_TILE
    row = lambda i: (i, 0)
    fixed = lambda i: (0, 0)
    return pl.pallas_call(
        _outproj_kernel,
        grid=(T // tm,),
        in_specs=[pl.BlockSpec((tm, ATTN_WIDTH), row), pl.BlockSpec((tm, RET_WIDTH), row),
                  pl.BlockSpec((tm, D_MODEL), row), pl.BlockSpec((D_MODEL, D_MODEL), fixed),
                  pl.BlockSpec((1, D_MODEL), fixed), pl.BlockSpec((D_MODEL, LANES), fixed)],
        out_specs=[pl.BlockSpec((tm, D_MODEL), row), pl.BlockSpec((tm, XAUG_COLS), row),
                   pl.BlockSpec((N_EXPERTS, tm), lambda i: (0, i))],
        out_shape=[jax.ShapeDtypeStruct((T, D_MODEL), f32), jax.ShapeDtypeStruct((T, XAUG_COLS), bf16),
                   jax.ShapeDtypeStruct((N_EXPERTS, T), f32)],
        compiler_params=_cparams(("arbitrary",)),
        name="outproj_router",
    )(a, r, x, w, g, wr)


SCAN_LANES = 512


def _thresh_kernel(a_ref, thr_ref, need_ref, *, cap):
    T = a_ref.shape[1]
    nchunk = T // SCAN_LANES

    def count_ge(cand):
        def body(c, acc):
            off = pl.multiple_of(c * SCAN_LANES, SCAN_LANES)
            keys = lax.bitcast_convert_type(a_ref[:, pl.ds(off, SCAN_LANES)], i32)
            return acc + (keys >= cand).astype(i32)
        acc = lax.fori_loop(0, nchunk, body, jnp.zeros((N_EXPERTS, SCAN_LANES), i32))
        return jnp.sum(acc, axis=1, keepdims=True)

    def bit_body(i, prefix):
        cand = prefix | (jnp.int32(1) << (30 - i))
        return jnp.where(count_ge(cand) >= cap, cand, prefix)

    thr = lax.fori_loop(0, 31, bit_body, jnp.zeros((N_EXPERTS, 1), i32))
    need = cap - count_ge(thr + 1)
    thr_ref[...] = jnp.broadcast_to(thr, thr_ref.shape)
    need_ref[...] = jnp.broadcast_to(need, need_ref.shape)


def _slots_kernel(a_ref, thr_ref, need_ref, slot_ref, slott_ref, start_ref, ceq_ref, csel_ref):
    @pl.when(pl.program_id(0) == 0)
    def _():
        ceq_ref[...] = jnp.zeros(ceq_ref.shape, f32)
        csel_ref[...] = jnp.zeros(csel_ref.shape, f32)

    P = TOKEN_TILE
    keys = lax.bitcast_convert_type(a_ref[...], i32)
    thr = thr_ref[:, 0:1]
    need = need_ref[:, 0:1].astype(f32)
    gt = keys > thr
    eq = keys == thr
    before = (lax.broadcasted_iota(i32, (P, P), 0) < lax.broadcasted_iota(i32, (P, P), 1)).astype(bf16)
    ceq = ceq_ref[:, 0:1]
    csel = csel_ref[:, 0:1]
    eqf = eq.astype(f32)
    eq_rank = jnp.dot(eqf.astype(bf16), before, preferred_element_type=f32) + ceq
    sel = gt | (eq & (eq_rank < need))
    self_ = sel.astype(f32)
    rank = jnp.dot(self_.astype(bf16), before, preferred_element_type=f32) + csel
    slot = jnp.where(sel, rank, -1.0)
    slot_ref[...] = slot.astype(i32)
    padded = jnp.concatenate([slot, jnp.full((LANES - N_EXPERTS, P), -1.0, f32)], axis=0)
    slott_ref[...] = padded.T.astype(i32)
    start_ref[0] = jnp.broadcast_to(csel, (N_EXPERTS, LANES)).astype(i32)
    ceq_ref[...] = jnp.broadcast_to(ceq + jnp.sum(eqf, axis=1, keepdims=True), ceq_ref.shape)
    csel_ref[...] = jnp.broadcast_to(csel + jnp.sum(self_, axis=1, keepdims=True), csel_ref.shape)


def _select(afft, cap):
    T = afft.shape[1]
    P = TOKEN_TILE
    nt = T // P
    thr, need = pl.pallas_call(
        functools.partial(_thresh_kernel, cap=cap),
        out_shape=[jax.ShapeDtypeStruct((N_EXPERTS, LANES), i32)] * 2,
        compiler_params=pltpu.CompilerParams(vmem_limit_bytes=VMEM_LIMIT),
        name="topk_threshold",
    )(afft)
    fixed = lambda i: (0, 0)
    slot, slott, starts = pl.pallas_call(
        _slots_kernel,
        grid=(nt,),
        in_specs=[pl.BlockSpec((N_EXPERTS, P), lambda i: (0, i)),
                  pl.BlockSpec((N_EXPERTS, LANES), fixed), pl.BlockSpec((N_EXPERTS, LANES), fixed)],
        out_specs=[pl.BlockSpec((N_EXPERTS, P), lambda i: (0, i)),
                   pl.BlockSpec((P, LANES), lambda i: (i, 0)),
                   pl.BlockSpec((1, N_EXPERTS, LANES), lambda i: (i, 0, 0))],
        out_shape=[jax.ShapeDtypeStruct((N_EXPERTS, T), i32), jax.ShapeDtypeStruct((T, LANES), i32),
                   jax.ShapeDtypeStruct((nt, N_EXPERTS, LANES), i32)],
        scratch_shapes=[pltpu.VMEM((N_EXPERTS, LANES), f32), pltpu.VMEM((N_EXPERTS, LANES), f32)],
        compiler_params=_cparams(("arbitrary",)),
        name="slot_ranks",
    )(afft, thr, need)
    starts = jnp.concatenate([starts[:, :, 0], jnp.full((1, N_EXPERTS), cap, i32)], axis=0)
    return slot, slott, starts.reshape(-1)


def _floor_tile(v):
    return (v // BF16_ROWS) * BF16_ROWS


def _dispatch_kernel(st_ref, x_ref, slot_ref, xe_ref, win_ref, stage_ref, sem):
    p = pl.program_id(0)
    nt = pl.num_programs(0)
    C = DISPATCH_ROWS
    P = TOKEN_TILE

    def window_copy(e, base):
        dst = xe_ref.at[e, pl.ds(pl.multiple_of(base, BF16_ROWS), C), :]
        return pltpu.make_async_copy(win_ref.at[e], dst, sem.at[e])

    @pl.when(p == 0)
    def _():
        stage_ref[...] = jnp.zeros(stage_ref.shape, bf16)
        win_ref[0] = jnp.zeros((C, XAUG_COLS), bf16)
        cap = xe_ref.shape[1] - C
        tails = [pltpu.make_async_copy(win_ref.at[0], xe_ref.at[e, pl.ds(cap, C), :], sem.at[e])
                 for e in range(N_EXPERTS)]
        for t in tails:
            t.start()
        for t in tails:
            t.wait()

    @pl.when(p > 0)
    def _():
        for e in range(N_EXPERTS):
            window_copy(e, 0).wait()

    x = x_ref[...]
    rows = lax.broadcasted_iota(i32, (C, P), 0)

    def onehot(e, base):
        return (rows == (slot_ref[e:e + 1, :] - base)).astype(bf16)

    start = [st_ref[p * N_EXPERTS + e] for e in range(N_EXPERTS)]
    stop = [st_ref[(p + 1) * N_EXPERTS + e] for e in range(N_EXPERTS)]
    base0 = [_floor_tile(s) for s in start]
    sel = jnp.concatenate([onehot(e, base0[e]) for e in range(N_EXPERTS)], axis=0)
    gathered = jnp.dot(sel, x, preferred_element_type=f32).astype(bf16)
    for e in range(N_EXPERTS):
        win_ref[e] = gathered[e * C:(e + 1) * C]
        win_ref[e, 0:BF16_ROWS, :] = gathered[e * C:e * C + BF16_ROWS] + stage_ref[e]
        window_copy(e, base0[e]).start()
        base_new = _floor_tile(stop[e])
        n_extra = (base_new - base0[e]) // C

        def extra(c, carry, e=e):
            base = base0[e] + c * C
            window_copy(e, 0).wait()
            win_ref[e] = jnp.dot(onehot(e, base), x, preferred_element_type=f32).astype(bf16)
            window_copy(e, base).start()
            return carry

        lax.fori_loop(1, n_extra + 1, extra, 0)
        off = pl.multiple_of(base_new - base0[e] - n_extra * C, BF16_ROWS)
        stage_ref[e] = win_ref[e, pl.ds(off, BF16_ROWS), :]

    @pl.when(p == nt - 1)
    def _():
        for e in range(N_EXPERTS):
            window_copy(e, 0).wait()


def _dispatch(xaug, slot, starts, cap):
    T = xaug.shape[0]
    P = TOKEN_TILE
    C = DISPATCH_ROWS
    grid_spec = pltpu.PrefetchScalarGridSpec(
        num_scalar_prefetch=1,
        grid=(T // P,),
        in_specs=[pl.BlockSpec((P, XAUG_COLS), lambda i, st: (i, 0)),
                  pl.BlockSpec((N_EXPERTS, P), lambda i, st: (0, i))],
        out_specs=pl.BlockSpec(memory_space=pl.ANY),
        scratch_shapes=[pltpu.VMEM((N_EXPERTS, C, XAUG_COLS), bf16),
                        pltpu.VMEM((N_EXPERTS, BF16_ROWS, XAUG_COLS), bf16),
                        pltpu.SemaphoreType.DMA((N_EXPERTS,))],
    )
    return pl.pallas_call(
        _dispatch_kernel,
        grid_spec=grid_spec,
        out_shape=jax.ShapeDtypeStruct((N_EXPERTS, cap + C, XAUG_COLS), bf16),
        compiler_params=_cparams(("arbitrary",)),
        name="dispatch",
    )(starts, xaug, slot)


FF_CHUNK = 512


def _ffn_kernel(xe_ref, wg_ref, wu_ref, wd_ref, ye_ref):
    e = pl.program_id(0)
    x = xe_ref[0, :, 0:D_MODEL]
    gsplit = xe_ref[0, :, D_MODEL:XAUG_COLS].astype(f32)
    lane = lax.broadcasted_iota(i32, gsplit.shape, 1)
    mine = (lane == e) | (lane == e + N_EXPERTS) | (lane == e + 2 * N_EXPERTS)
    gate = jnp.sum(jnp.where(mine, gsplit, 0.0), axis=-1, keepdims=True)
    acc = jnp.zeros((x.shape[0], D_MODEL), f32)
    for f in range(0, D_FF, FF_CHUNK):
        g = jnp.dot(x, wg_ref[0, :, f:f + FF_CHUNK], preferred_element_type=f32)
        u = jnp.dot(x, wu_ref[0, :, f:f + FF_CHUNK], preferred_element_type=f32)
        h = (g * _sigmoid(g) * u).astype(bf16)
        acc = acc + jnp.dot(h, wd_ref[0, f:f + FF_CHUNK, :], preferred_element_type=f32)
    ye_ref[0] = (acc * gate).astype(bf16)


def _ffn(xe, wg, wu, wd, cap):
    tm = min(PROJ_TILE, cap)
    return pl.pallas_call(
        _ffn_kernel,
        grid=(N_EXPERTS, cap // tm),
        in_specs=[pl.BlockSpec((1, tm, XAUG_COLS), lambda e, j: (e, j, 0)),
                  pl.BlockSpec((1, D_MODEL, D_FF), lambda e, j: (e, 0, 0)),
                  pl.BlockSpec((1, D_MODEL, D_FF), lambda e, j: (e, 0, 0)),
                  pl.BlockSpec((1, D_FF, D_MODEL), lambda e, j: (e, 0, 0))],
        out_specs=pl.BlockSpec((1, tm, D_MODEL), lambda e, j: (e, j, 0)),
        out_shape=jax.ShapeDtypeStruct((N_EXPERTS, cap, D_MODEL), bf16),
        compiler_params=_cparams(("arbitrary", "arbitrary")),
        name="experts",
    )(xe, wg, wu, wd)


def _combine_kernel(st_ref, x1_ref, slott_ref, ye_ref, o_ref, ywin_ref, sem, *, cap, final_norm):
    p = pl.program_id(0)
    W = COMBINE_ROWS
    P = TOKEN_TILE
    start = [st_ref[p * N_EXPERTS + e] for e in range(N_EXPERTS)]
    stop = [st_ref[(p + 1) * N_EXPERTS + e] for e in range(N_EXPERTS)]
    base0 = [_floor_tile(s) for s in start]
    lanes = lax.broadcasted_iota(i32, (P, W), 1)
    slots = slott_ref[...]

    def window_copy(e, lo):
        src_lo = pl.multiple_of(jnp.minimum(lo, cap - W), BF16_ROWS)
        return pltpu.make_async_copy(ye_ref.at[e, pl.ds(src_lo, W), :],
                                     ywin_ref.at[pl.ds(e * W, W), :], sem.at[e])

    def gathered(c):
        los = [base0[e] + c * W for e in range(N_EXPERTS)]
        for e in range(N_EXPERTS):
            window_copy(e, los[e]).start()
        pieces = []
        for e in range(N_EXPERTS):
            col = slots[:, e:e + 1]
            inside = (col >= los[e]) & (col < los[e] + W)
            w = jnp.where(inside, col - jnp.minimum(los[e], cap - W), -1)
            pieces.append((lanes == w).astype(bf16))
        comb = jnp.concatenate(pieces, axis=1)
        for e in range(N_EXPERTS):
            window_copy(e, los[e]).wait()
        return jnp.dot(comb, ywin_ref[...], preferred_element_type=f32)

    o_ref[...] = x1_ref[...] + gathered(0)
    n_extra = jnp.int32(0)
    for e in range(N_EXPERTS):
        last = jnp.maximum(stop[e] - 1, base0[e])
        n_extra = jnp.maximum(n_extra, (last - base0[e]) // W)

    def extra(c, carry):
        o_ref[...] = o_ref[...] + gathered(c)
        return carry

    lax.fori_loop(1, n_extra + 1, extra, 0)
    if final_norm is not None:
        x2 = o_ref[...]
        o_ref[...] = x2 * lax.rsqrt(jnp.mean(x2 * x2, axis=-1, keepdims=True) + EPS) * final_norm[...]


def _combine_body(st_ref, x1_ref, slott_ref, ye_ref, o_ref, ywin_ref, sem, *, cap):
    _combine_kernel(st_ref, x1_ref, slott_ref, ye_ref, o_ref, ywin_ref, sem, cap=cap, final_norm=None)


def _combine_norm_body(st_ref, x1_ref, slott_ref, ye_ref, g_ref, o_ref, ywin_ref, sem, *, cap):
    _combine_kernel(st_ref, x1_ref, slott_ref, ye_ref, o_ref, ywin_ref, sem, cap=cap, final_norm=g_ref)


def _combine(x1, slott, ye, starts, cap, final_g=None):
    T = x1.shape[0]
    P = TOKEN_TILE
    in_specs = [pl.BlockSpec((P, D_MODEL), lambda i, st: (i, 0)),
                pl.BlockSpec((P, LANES), lambda i, st: (i, 0)),
                pl.BlockSpec(memory_space=pl.ANY)]
    args = [starts, x1, slott, ye]
    body = _combine_body
    if final_g is not None:
        in_specs.append(pl.BlockSpec((1, D_MODEL), lambda i, st: (0, 0)))
        args.append(final_g)
        body = _combine_norm_body
    grid_spec = pltpu.PrefetchScalarGridSpec(
        num_scalar_prefetch=1,
        grid=(T // P,),
        in_specs=in_specs,
        out_specs=pl.BlockSpec((P, D_MODEL), lambda i, st: (i, 0)),
        scratch_shapes=[pltpu.VMEM((N_EXPERTS * COMBINE_ROWS, D_MODEL), bf16),
                        pltpu.SemaphoreType.DMA((N_EXPERTS,))],
    )
    return pl.pallas_call(
        functools.partial(body, cap=cap),
        grid_spec=grid_spec,
        out_shape=jax.ShapeDtypeStruct((T, D_MODEL), f32),
        compiler_params=_cparams(("arbitrary",)),
        name="combine",
    )(*args)


def _rope_tables(seq):
    half = ROPE_DIMS // 2
    pos = jnp.arange(seq, dtype=f32)
    inv_freq = jnp.exp(-jnp.log(jnp.float32(ROPE_THETA)) * jnp.arange(half, dtype=f32) * (2.0 / ROPE_DIMS))
    ang = pos[:, None] * inv_freq[None, :]
    cos, sin = jnp.cos(ang), jnp.sin(ang)
    rest = HEAD_DIM - ROPE_DIMS
    ones = jnp.ones((seq, rest), f32)
    zeros_h = jnp.zeros((seq, half), f32)
    zeros_r = jnp.zeros((seq, rest), f32)
    c = jnp.concatenate([cos, cos, ones], axis=1)
    s1 = jnp.concatenate([zeros_h, sin, zeros_r], axis=1)
    s2 = jnp.concatenate([-sin, zeros_h, zeros_r], axis=1)
    rep = LANES // HEAD_DIM
    return jnp.tile(c, (1, rep)), jnp.tile(s1, (1, rep)), jnp.tile(s2, (1, rep))


def _per_lane(v):
    return jnp.repeat(v.astype(f32), HEAD_DIM)[None, :]


def _trunk(x, params):
    batch, seq, _ = x.shape
    T = batch * seq
    cap = CAPACITY_FACTOR * T // N_EXPERTS
    assert seq % (RET_CHUNK * RET_CHUNKS_PER_STEP) == 0 and seq % PROJ_TILE == 0
    assert cap % COMBINE_ROWS == 0 and cap % min(PROJ_TILE, cap) == 0
    x = x.reshape(T, D_MODEL)
    rc, rs1, rs2 = _rope_tables(seq)
    depth = params["w_in"].shape[0]
    for l in range(depth):
        q, kv, ret = _inproj(x, params["norm1_g"][l][None, :], params["w_in"][l], rc, rs1, rs2, seq)
        sink_rows = jnp.broadcast_to(params["attn_sink"][l].astype(f32)[:, None], (N_ATTN_HEADS, LANES))
        a_out = _attention(q, kv, sink_rows, params["attn_out_g"][l][None, :], batch, seq)
        r_out = _retention(ret, _per_lane(params["ret_decay_fwd"][l]), _per_lane(params["ret_decay_bwd"][l]),
                           params["ret_out_g"][l][None, :], batch, seq)
        x1, xaug, afft = _outproj(a_out, r_out, x, params["w_out"][l], params["norm2_g"][l][None, :],
                                  params["w_router"][l])
        slot, slott, starts = _select(afft, cap)
        xe = _dispatch(xaug, slot, starts, cap)
        ye = _ffn(xe, params["w_gate"][l], params["w_up"][l], params["w_down"][l], cap)
        final_g = params["final_norm_g"][None, :] if l == depth - 1 else None
        x = _combine(x1, slott, ye, starts, cap, final_g)
    return x.reshape(batch, seq, D_MODEL)


def kernel(x_prompt, x_sample, norm1_g, w_in, attn_sink, ret_decay_fwd, ret_decay_bwd, attn_out_g, ret_out_g,
           w_out, norm2_g, w_router, w_gate, w_up, w_down, final_norm_g):
    wr = jnp.pad(w_router, ((0, 0), (0, 0), (0, LANES - N_EXPERTS)))
    params = dict(
        norm1_g=norm1_g, w_in=w_in.astype(bf16), attn_sink=attn_sink, ret_decay_fwd=ret_decay_fwd,
        ret_decay_bwd=ret_decay_bwd, attn_out_g=attn_out_g, ret_out_g=ret_out_g, w_out=w_out.astype(bf16),
        norm2_g=norm2_g, w_router=wr.astype(bf16), w_gate=w_gate.astype(bf16), w_up=w_up.astype(bf16),
        w_down=w_down.astype(bf16), final_norm_g=final_norm_g)
    return (_trunk(x_prompt, params), _trunk(x_sample, params))
```

```python
import functools

import jax
import jax.numpy as jnp
from jax import lax
from jax.experimental import pallas as pl
from jax.experimental.pallas import tpu as pltpu

f32 = jnp.float32
bf16 = jnp.bfloat16
i32 = jnp.int32

D_MODEL = 1024
HEAD_DIM = 64
N_ATTN_HEADS = 8
N_KV_HEADS = 2
N_RET_HEADS = 8
ATTN_WIDTH = N_ATTN_HEADS * HEAD_DIM
KV_WIDTH = N_KV_HEADS * HEAD_DIM
RET_WIDTH = N_RET_HEADS * HEAD_DIM
QKV_COLS = ATTN_WIDTH + 2 * KV_WIDTH
IN_COLS = QKV_COLS + 4 * RET_WIDTH
WINDOW = 128
BLOCK = 128
RET_CHUNK = 128
ROPE_THETA = 500000.0
ROPE_DIMS = HEAD_DIM // 4
N_EXPERTS = 16
CAPACITY_FACTOR = 2
D_FF = 2 * D_MODEL
EPS = 1e-6
NEG = -1e30

LANES = 128
BF16_ROWS = 16
GATE_COLS = LANES
XAUG_COLS = D_MODEL + GATE_COLS
TOKEN_TILE = 256
DISPATCH_CHUNK = 256
COMBINE_CHUNK = 512
STACK_ROWS = N_EXPERTS * (TOKEN_TILE + 2 * BF16_ROWS)
STACK_ROWS = -(-STACK_ROWS // COMBINE_CHUNK) * COMBINE_CHUNK
PROJ_TILE = 512
RANK_TILES = 4
ATTN_QBLOCKS = 4
RET_CHUNKS_PER_STEP = 4
VMEM_LIMIT = 48 * 1024 * 1024


def _cparams(sem):
    return pltpu.CompilerParams(dimension_semantics=sem, vmem_limit_bytes=VMEM_LIMIT)


def _sigmoid(x):
    return 1.0 / (1.0 + jnp.exp(-x))


def _log_sigmoid(x):
    return jnp.minimum(x, 0.0) - jnp.log(1.0 + jnp.exp(-jnp.abs(x)))


def _inproj_kernel(x_ref, g_ref, w_ref, c_ref, s1_ref, s2_ref, q_ref, kv_ref, ret_ref):
    x = x_ref[...]
    h = x * lax.rsqrt(jnp.mean(x * x, axis=-1, keepdims=True) + EPS) * g_ref[...]
    hb = h.astype(bf16)
    c, s1, s2 = c_ref[...], s1_ref[...], s2_ref[...]

    def rope(p):
        half = ROPE_DIMS // 2
        return p * c + pltpu.roll(p, half, 1) * s1 + pltpu.roll(p, LANES - half, 1) * s2

    qkv = jnp.dot(hb, w_ref[0, :, 0:QKV_COLS], preferred_element_type=f32)
    for j in range(ATTN_WIDTH // LANES):
        q_ref[:, j * LANES:(j + 1) * LANES] = rope(qkv[:, j * LANES:(j + 1) * LANES]).astype(bf16)
    kv_ref[:, 0:KV_WIDTH] = rope(qkv[:, ATTN_WIDTH:ATTN_WIDTH + KV_WIDTH]).astype(bf16)
    kv_ref[:, KV_WIDTH:2 * KV_WIDTH] = qkv[:, ATTN_WIDTH + KV_WIDTH:QKV_COLS].astype(bf16)
    for j in range(4):
        lo = QKV_COLS + j * RET_WIDTH
        ret_ref[:, j * RET_WIDTH:(j + 1) * RET_WIDTH] = jnp.dot(
            hb, w_ref[0, :, lo:lo + RET_WIDTH], preferred_element_type=f32).astype(bf16)


def _inproj(x, g, w, layer, rc, rs1, rs2, seq):
    T = x.shape[0]
    tm = PROJ_TILE
    npos = seq // tm
    row = lambda i: (i, 0)
    pos = lambda i: (i % npos, 0)
    fixed = lambda i: (0, 0)
    return pl.pallas_call(
        _inproj_kernel,
        grid=(T // tm,),
        in_specs=[pl.BlockSpec((tm, D_MODEL), row), pl.BlockSpec((1, D_MODEL), fixed),
                  pl.BlockSpec((1, D_MODEL, IN_COLS), lambda i: (layer, 0, 0)),
                  pl.BlockSpec((tm, LANES), pos), pl.BlockSpec((tm, LANES), pos),
                  pl.BlockSpec((tm, LANES), pos)],
        out_specs=[pl.BlockSpec((tm, ATTN_WIDTH), row), pl.BlockSpec((tm, 2 * KV_WIDTH), row),
                   pl.BlockSpec((tm, 4 * RET_WIDTH), row)],
        out_shape=[jax.ShapeDtypeStruct((T, ATTN_WIDTH), bf16),
                   jax.ShapeDtypeStruct((T, 2 * KV_WIDTH), bf16),
                   jax.ShapeDtypeStruct((T, 4 * RET_WIDTH), bf16)],
        compiler_params=_cparams(("arbitrary",)),
        name="inproj",
    )(x, g, w, rc, rs1, rs2)


def _attn_kernel(q_ref, kp_ref, kc_ref, kn_ref, sink_ref, g_ref, o_ref):
    n = pl.program_id(1)
    nsteps = pl.num_programs(1)
    QB = ATTN_QBLOCKS
    kv = jnp.concatenate([kp_ref[...], kc_ref[...], kn_ref[...]], axis=0)
    rows = kv.shape[0]
    lane = lax.broadcasted_iota(i32, (rows, LANES), 1)
    low = lane < HEAD_DIM
    kf = kv[:, 0:LANES].astype(f32) * (HEAD_DIM ** -0.5)
    kr = pltpu.roll(kf, HEAD_DIM, 1)
    vf = kv[:, LANES:2 * LANES].astype(f32)
    vr = pltpu.roll(vf, HEAD_DIM, 1)
    ones_hi = (lane == HEAD_DIM).astype(f32)
    ones_lo = (lane == 0).astype(f32)
    k_low = [jnp.where(low, kf, 0.0).astype(bf16), jnp.where(low, kr, 0.0).astype(bf16)]
    k_high = [jnp.where(low, 0.0, kr).astype(bf16), jnp.where(low, 0.0, kf).astype(bf16)]
    v_low = [jnp.where(low, vf, ones_hi).astype(bf16), jnp.where(low, vr, ones_hi).astype(bf16)]
    v_high = [jnp.where(low, ones_lo, vr).astype(bf16), jnp.where(low, ones_lo, vf).astype(bf16)]

    qi = lax.broadcasted_iota(i32, (BLOCK, 3 * BLOCK), 0)
    ki = lax.broadcasted_iota(i32, (BLOCK, 3 * BLOCK), 1)
    in_window = jnp.abs(ki - BLOCK - qi) <= WINDOW
    lowq = lax.broadcasted_iota(i32, (BLOCK, LANES), 1) < HEAD_DIM
    grp = N_ATTN_HEADS // N_KV_HEADS
    g_row = g_ref[...]
    def scores(i):
        blk = n * QB + i
        valid = in_window & ((ki >= BLOCK) | (blk > 0)) & ((ki < 2 * BLOCK) | (blk < nsteps * QB - 1))
        bias = jnp.where(valid, 0.0, NEG)
        keys = slice(i * BLOCK, (i + 3) * BLOCK)
        out = []
        for h in range(N_ATTN_HEADS):
            qp = q_ref[i * BLOCK:(i + 1) * BLOCK, (h // 2) * LANES:(h // 2 + 1) * LANES]
            kx = (k_high if h % 2 else k_low)[h // grp]
            s = lax.dot_general(qp, kx[keys], (((1,), (1,)), ((), ())), preferred_element_type=f32)
            out.append(jnp.concatenate([s[:, 0:BLOCK] + bias[:, 0:BLOCK], s[:, BLOCK:2 * BLOCK],
                                        s[:, 2 * BLOCK:] + bias[:, 2 * BLOCK:]], axis=1))
        return out

    pending = scores(0)
    for i in range(QB):
        current = pending
        if i + 1 < QB:
            pending = scores(i + 1)
        keys = slice(i * BLOCK, (i + 3) * BLOCK)
        probs, maxes = [], []
        for h in range(N_ATTN_HEADS):
            sk = sink_ref[h:h + 1, 0:1]
            m = jnp.maximum(jnp.max(current[h], axis=-1, keepdims=True), sk)
            probs.append(jnp.exp(current[h] - m).astype(bf16))
            maxes.append(m)
        halves = []
        for h in range(N_ATTN_HEADS):
            vx = (v_high if h % 2 else v_low)[h // grp]
            ones_at = 0 if h % 2 else HEAD_DIM
            oa = jnp.dot(probs[h], vx[keys], preferred_element_type=f32)
            denom = oa[:, ones_at:ones_at + 1] + jnp.exp(sink_ref[h:h + 1, 0:1] - maxes[h])
            halves.append(oa / denom)
        pieces = [jnp.where(lowq, halves[2 * j], halves[2 * j + 1]) for j in range(N_ATTN_HEADS // 2)]
        a = jnp.concatenate(pieces, axis=1)
        a = a * lax.rsqrt(jnp.mean(a * a, axis=-1, keepdims=True) + EPS) * g_row
        o_ref[i * BLOCK:(i + 1) * BLOCK, :] = a.astype(bf16)


def _attention(q, kv, sink_rows, g, batch, seq):
    T = q.shape[0]
    QB = ATTN_QBLOCKS
    nb = seq // BLOCK
    ns = nb // QB
    cur = lambda b, n: (b * ns + n, 0)
    prev = lambda b, n: (b * nb + jnp.maximum(n * QB - 1, 0), 0)
    nxt = lambda b, n: (b * nb + jnp.minimum(n * QB + QB, nb - 1), 0)
    fixed = lambda b, n: (0, 0)
    return pl.pallas_call(
        _attn_kernel,
        grid=(batch, ns),
        in_specs=[pl.BlockSpec((QB * BLOCK, ATTN_WIDTH), cur),
                  pl.BlockSpec((BLOCK, 2 * KV_WIDTH), prev),
                  pl.BlockSpec((QB * BLOCK, 2 * KV_WIDTH), cur),
                  pl.BlockSpec((BLOCK, 2 * KV_WIDTH), nxt),
                  pl.BlockSpec((N_ATTN_HEADS, LANES), fixed),
                  pl.BlockSpec((1, ATTN_WIDTH), fixed)],
        out_specs=pl.BlockSpec((QB * BLOCK, ATTN_WIDTH), cur),
        out_shape=jax.ShapeDtypeStruct((T, ATTN_WIDTH), bf16),
        compiler_params=_cparams(("arbitrary", "arbitrary")),
        name="attention",
    )(q, kv, kv, kv, sink_rows, g)


N_PAIRS = RET_WIDTH // LANES


def _pair_masks():
    r = lax.broadcasted_iota(i32, (LANES, LANES), 0)
    c = lax.broadcasted_iota(i32, (LANES, LANES), 1)
    low = c < HEAD_DIM
    blockdiag = (r < HEAD_DIM) == (c < HEAD_DIM)
    return low, blockdiag


def _ret_bstate_kernel(k_ref, v_ref, db_ref, sb_ref, st_ref):
    @pl.when(pl.program_id(1) == 0)
    def _():
        st_ref[...] = jnp.zeros(st_ref.shape, f32)

    C = RET_CHUNK
    lg = _log_sigmoid(db_ref[...])
    idx = lax.broadcasted_iota(i32, (C, RET_WIDTH), 0).astype(f32)
    zeta = jnp.exp(lg * idx)
    gc = jnp.exp(lg * float(C))
    _, blockdiag = _pair_masks()
    for c in reversed(range(RET_CHUNKS_PER_STEP)):
        rows = slice(c * C, (c + 1) * C)
        for j in range(N_PAIRS):
            lanes = slice(j * LANES, (j + 1) * LANES)
            kp = k_ref[rows, lanes].astype(f32) * (HEAD_DIM ** -0.5)
            vp = v_ref[rows, lanes]
            st = st_ref[j]
            sb_ref[c, j] = st.astype(bf16)
            kz = (kp * zeta[:, lanes]).T.astype(bf16)
            upd = jnp.dot(kz, vp, preferred_element_type=f32)
            st_ref[j] = jnp.where(blockdiag, gc[:, lanes] * st + upd, 0.0)


def _ret_main_kernel(q_ref, k_ref, v_ref, gate_ref, sb_ref, df_ref, db_ref, gn_ref, o_ref, st_ref):
    @pl.when(pl.program_id(1) == 0)
    def _():
        st_ref[...] = jnp.zeros(st_ref.shape, f32)

    C = RET_CHUNK
    lgf = _log_sigmoid(df_ref[...])
    lgb = _log_sigmoid(db_ref[...])
    idx = lax.broadcasted_iota(i32, (C, RET_WIDTH), 0).astype(f32)
    xi_f = jnp.exp(lgf * (idx + 1.0))
    xi_b = jnp.exp(lgb * (float(C) - idx))
    zeta_f = jnp.exp(lgf * (float(C - 1) - idx))
    gc_f = jnp.exp(lgf * float(C))
    low, blockdiag = _pair_masks()
    diff = (lax.broadcasted_iota(i32, (C, C), 0) - lax.broadcasted_iota(i32, (C, C), 1)).astype(f32)

    def head_decay(h):
        a = lgf[:, h * HEAD_DIM:h * HEAD_DIM + 1]
        b = lgb[:, h * HEAD_DIM:h * HEAD_DIM + 1]
        return jnp.where(diff >= 0.0, jnp.exp(a * jnp.maximum(diff, 0.0)),
                         jnp.exp(b * jnp.maximum(-diff, 0.0)))

    decay = [jnp.concatenate([head_decay(2 * j), head_decay(2 * j + 1)], axis=0) for j in range(N_PAIRS)]
    gn = gn_ref[...]
    inv_hd = 1.0 / HEAD_DIM

    for c in range(RET_CHUNKS_PER_STEP):
        rows = slice(c * C, (c + 1) * C)
        for j in range(N_PAIRS):
            lanes = slice(j * LANES, (j + 1) * LANES)
            qp = q_ref[rows, lanes]
            qf = qp.astype(f32)
            kp = k_ref[rows, lanes].astype(f32) * (HEAD_DIM ** -0.5)
            kb = kp.astype(bf16)
            vp = v_ref[rows, lanes]
            zero = jnp.zeros_like(qp)
            q2 = jnp.concatenate([jnp.where(low, qp, zero), jnp.where(low, zero, qp)], axis=0)
            s2 = lax.dot_general(q2, kb, (((1,), (1,)), ((), ())), preferred_element_type=f32)
            o2 = jnp.dot((s2 * decay[j]).astype(bf16), vp, preferred_element_type=f32)
            inner = jnp.where(low, o2[0:C], o2[C:2 * C])
            st = st_ref[j]
            cross_f = jnp.dot((qf * xi_f[:, lanes]).astype(bf16), st.astype(bf16),
                              preferred_element_type=f32)
            cross_b = jnp.dot((qf * xi_b[:, lanes]).astype(bf16), sb_ref[c, j],
                              preferred_element_type=f32)
            o = inner + cross_f + cross_b
            s_lo = jnp.sum(jnp.where(low, o, 0.0), axis=-1, keepdims=True)
            s_hi = jnp.sum(jnp.where(low, 0.0, o), axis=-1, keepdims=True)
            d = o - jnp.where(low, s_lo, s_hi) * inv_hd
            dd = d * d
            v_lo = jnp.sum(jnp.where(low, dd, 0.0), axis=-1, keepdims=True)
            v_hi = jnp.sum(jnp.where(low, 0.0, dd), axis=-1, keepdims=True)
            var = jnp.where(low, v_lo, v_hi) * inv_hd
            on = d * lax.rsqrt(var + EPS) * gn[:, lanes]
            gate = gate_ref[rows, lanes].astype(f32)
            o_ref[rows, lanes] = (gate * _sigmoid(gate) * on).astype(bf16)
            kz = (kp * zeta_f[:, lanes]).T.astype(bf16)
            upd = jnp.dot(kz, vp, preferred_element_type=f32)
            st_ref[j] = jnp.where(blockdiag, gc_f[:, lanes] * st + upd, 0.0)


def _retention(ret, dec_f, dec_b, gn, batch, seq):
    T = ret.shape[0]
    C = RET_CHUNK
    ch = RET_CHUNKS_PER_STEP
    nblk = seq // (C * ch)
    rows = ch * C
    fixed = lambda b, i: (0, 0)
    col = lambda j: (lambda b, i: (b * nblk + i, j))
    rcol = lambda j: (lambda b, i: (b * nblk + nblk - 1 - i, j))
    sb = pl.pallas_call(
        _ret_bstate_kernel,
        grid=(batch, nblk),
        in_specs=[pl.BlockSpec((rows, RET_WIDTH), rcol(1)), pl.BlockSpec((rows, RET_WIDTH), rcol(2)),
                  pl.BlockSpec((1, RET_WIDTH), fixed)],
        out_specs=pl.BlockSpec((ch, N_PAIRS, LANES, LANES), lambda b, i: (b * nblk + nblk - 1 - i, 0, 0, 0)),
        out_shape=jax.ShapeDtypeStruct((T // C, N_PAIRS, LANES, LANES), bf16),
        scratch_shapes=[pltpu.VMEM((N_PAIRS, LANES, LANES), f32)],
        compiler_params=_cparams(("arbitrary", "arbitrary")),
        name="ret_bstate",
    )(ret, ret, dec_b)
    return pl.pallas_call(
        _ret_main_kernel,
        grid=(batch, nblk),
        in_specs=[pl.BlockSpec((rows, RET_WIDTH), col(0)), pl.BlockSpec((rows, RET_WIDTH), col(1)),
                  pl.BlockSpec((rows, RET_WIDTH), col(2)), pl.BlockSpec((rows, RET_WIDTH), col(3)),
                  pl.BlockSpec((ch, N_PAIRS, LANES, LANES), lambda b, i: (b * nblk + i, 0, 0, 0)),
                  pl.BlockSpec((1, RET_WIDTH), fixed), pl.BlockSpec((1, RET_WIDTH), fixed),
                  pl.BlockSpec((1, RET_WIDTH), fixed)],
        out_specs=pl.BlockSpec((rows, RET_WIDTH), col(0)),
        out_shape=jax.ShapeDtypeStruct((T, RET_WIDTH), bf16),
        scratch_shapes=[pltpu.VMEM((N_PAIRS, LANES, LANES), f32)],
        compiler_params=_cparams(("arbitrary", "arbitrary")),
        name="ret_main",
    )(ret, ret, ret, ret, sb, dec_f, dec_b, gn)


def _outproj_kernel(a_ref, r_ref, x_ref, w_ref, g_ref, wr_ref, x1_ref, xaug_ref, afft_ref):
    mixed = jnp.concatenate([a_ref[...], r_ref[...]], axis=1)
    x1 = x_ref[...] + jnp.dot(mixed, w_ref[0], preferred_element_type=f32)
    x1_ref[...] = x1
    h2 = x1 * lax.rsqrt(jnp.mean(x1 * x1, axis=-1, keepdims=True) + EPS) * g_ref[...]
    hb = h2.astype(bf16)
    xaug_ref[:, 0:D_MODEL] = hb
    logits = jnp.dot(hb, wr_ref[0], preferred_element_type=f32)
    lane = lax.broadcasted_iota(i32, logits.shape, 1)
    real = lane < N_EXPERTS
    lg = jnp.where(real, logits, NEG)
    e = jnp.where(real, jnp.exp(lg - jnp.max(lg, axis=-1, keepdims=True)), 0.0)
    aff = e / jnp.sum(e, axis=-1, keepdims=True)
    afft_ref[...] = aff.T[0:N_EXPERTS, :]
    hi = aff.astype(bf16).astype(f32)
    r1 = aff - hi
    mid = r1.astype(bf16).astype(f32)
    lo = (r1 - mid).astype(bf16).astype(f32)
    cols = hi + pltpu.roll(mid, N_EXPERTS, 1) + pltpu.roll(lo, 2 * N_EXPERTS, 1)
    xaug_ref[:, D_MODEL:XAUG_COLS] = cols.astype(bf16)


def _outproj(a, r, x, w, g, wr, layer):
    T = x.shape[0]
    tm = PROJ_TILE
    row = lambda i: (i, 0)
    fixed = lambda i: (0, 0)
    lay = lambda i: (layer, 0, 0)
    return pl.pallas_call(
        _outproj_kernel,
        grid=(T // tm,),
        in_specs=[pl.BlockSpec((tm, ATTN_WIDTH), row), pl.BlockSpec((tm, RET_WIDTH), row),
                  pl.BlockSpec((tm, D_MODEL), row), pl.BlockSpec((1, D_MODEL, D_MODEL), lay),
                  pl.BlockSpec((1, D_MODEL), fixed), pl.BlockSpec((1, D_MODEL, LANES), lay)],
        out_specs=[pl.BlockSpec((tm, D_MODEL), row), pl.BlockSpec((tm, XAUG_COLS), row),
                   pl.BlockSpec((N_EXPERTS, tm), lambda i: (0, i))],
        out_shape=[jax.ShapeDtypeStruct((T, D_MODEL), f32), jax.ShapeDtypeStruct((T, XAUG_COLS), bf16),
                   jax.ShapeDtypeStruct((N_EXPERTS, T), f32)],
        compiler_params=_cparams(("arbitrary",)),
        name="outproj_router",
    )(a, r, x, w, g, wr)


SCAN_LANES = 512


def _thresh_kernel(a_ref, thr_ref, need_ref, *, cap):
    T = a_ref.shape[1]
    nchunk = T // SCAN_LANES

    def count_ge(cand):
        def body(c, acc):
            off = pl.multiple_of(c * SCAN_LANES, SCAN_LANES)
            keys = lax.bitcast_convert_type(a_ref[:, pl.ds(off, SCAN_LANES)], i32)
            return acc + (keys >= cand).astype(i32)
        acc = lax.fori_loop(0, nchunk, body, jnp.zeros((N_EXPERTS, SCAN_LANES), i32))
        return jnp.sum(acc, axis=1, keepdims=True)

    def bit_body(i, prefix):
        cand = prefix | (jnp.int32(1) << (30 - i))
        return jnp.where(count_ge(cand) >= cap, cand, prefix)

    thr = lax.fori_loop(0, 31, bit_body, jnp.zeros((N_EXPERTS, 1), i32))
    need = cap - count_ge(thr + 1)
    thr_ref[...] = jnp.broadcast_to(thr, thr_ref.shape)
    need_ref[...] = jnp.broadcast_to(need, need_ref.shape)


def _slots_kernel(a_ref, thr_ref, need_ref, slot_ref, slott_ref, start_ref, ceq_ref, csel_ref):
    @pl.when(pl.program_id(0) == 0)
    def _():
        ceq_ref[...] = jnp.zeros(ceq_ref.shape, f32)
        csel_ref[...] = jnp.zeros(csel_ref.shape, f32)

    P = TOKEN_TILE
    thr = thr_ref[:, 0:1]
    need = need_ref[:, 0:1].astype(f32)
    before = (lax.broadcasted_iota(i32, (P, P), 0) < lax.broadcasted_iota(i32, (P, P), 1)).astype(bf16)
    ceq = ceq_ref[:, 0:1]
    csel = csel_ref[:, 0:1]
    for j in range(a_ref.shape[1] // P):
        cols = slice(j * P, (j + 1) * P)
        keys = lax.bitcast_convert_type(a_ref[:, cols], i32)
        gt = keys > thr
        eq = keys == thr
        eqf = eq.astype(f32)
        eq_rank = jnp.dot(eqf.astype(bf16), before, preferred_element_type=f32) + ceq
        sel = gt | (eq & (eq_rank < need))
        self_ = sel.astype(f32)
        rank = jnp.dot(self_.astype(bf16), before, preferred_element_type=f32) + csel
        slot = jnp.where(sel, rank, -1.0)
        slot_ref[:, cols] = slot.astype(i32)
        padded = jnp.concatenate([slot, jnp.full((LANES - N_EXPERTS, P), -1.0, f32)], axis=0)
        slott_ref[cols, :] = padded.T.astype(i32)
        start_ref[j] = jnp.broadcast_to(csel, (N_EXPERTS, LANES)).astype(i32)
        ceq = ceq + jnp.sum(eqf, axis=1, keepdims=True)
        csel = csel + jnp.sum(self_, axis=1, keepdims=True)
    ceq_ref[...] = jnp.broadcast_to(ceq, ceq_ref.shape)
    csel_ref[...] = jnp.broadcast_to(csel, csel_ref.shape)


def _select(afft, cap):
    T = afft.shape[1]
    P = TOKEN_TILE
    nt = T // P
    rt = min(RANK_TILES, nt)
    thr, need = pl.pallas_call(
        functools.partial(_thresh_kernel, cap=cap),
        out_shape=[jax.ShapeDtypeStruct((N_EXPERTS, LANES), i32)] * 2,
        compiler_params=pltpu.CompilerParams(vmem_limit_bytes=VMEM_LIMIT),
        name="topk_threshold",
    )(afft)
    fixed = lambda i: (0, 0)
    slot, slott, starts = pl.pallas_call(
        _slots_kernel,
        grid=(nt // rt,),
        in_specs=[pl.BlockSpec((N_EXPERTS, rt * P), lambda i: (0, i)),
                  pl.BlockSpec((N_EXPERTS, LANES), fixed), pl.BlockSpec((N_EXPERTS, LANES), fixed)],
        out_specs=[pl.BlockSpec((N_EXPERTS, rt * P), lambda i: (0, i)),
                   pl.BlockSpec((rt * P, LANES), lambda i: (i, 0)),
                   pl.BlockSpec((rt, N_EXPERTS, LANES), lambda i: (i, 0, 0))],
        out_shape=[jax.ShapeDtypeStruct((N_EXPERTS, T), i32), jax.ShapeDtypeStruct((T, LANES), i32),
                   jax.ShapeDtypeStruct((nt, N_EXPERTS, LANES), i32)],
        scratch_shapes=[pltpu.VMEM((N_EXPERTS, LANES), f32), pltpu.VMEM((N_EXPERTS, LANES), f32)],
        compiler_params=_cparams(("arbitrary",)),
        name="slot_ranks",
    )(afft, thr, need)
    starts = jnp.concatenate([starts[:, :, 0], jnp.full((1, N_EXPERTS), cap, i32)], axis=0)
    return slot, slott, starts.reshape(-1)


BIG = 1 << 30


ROW_SHIFT = BF16_ROWS.bit_length() - 1


def _floor_tile(v):
    return (v >> ROW_SHIFT) << ROW_SHIFT


def _ceil_div_pow2(v, d):
    return (v + d - 1) >> (d.bit_length() - 1)


def _expert_vector(vals, shape, axis, fill=0):
    idx = lax.broadcasted_iota(i32, shape, axis)
    out = jnp.full(shape, fill, i32)
    for e, v in enumerate(vals):
        out = jnp.where(idx == e, v, out)
    return out


def _prefix(lengths):
    offs = [jnp.int32(0)]
    for n in lengths:
        offs.append(offs[-1] + n)
    return offs


def _split_bf16(tgt):
    hi = jnp.where(tgt >= 0, (tgt >> 6) << 6, 0)
    lo = jnp.where(tgt >= 0, tgt & 63, -1)
    return hi.astype(f32).astype(bf16), lo.astype(f32).astype(bf16)


def _dispatch_kernel(st_ref, x_ref, slot_ref, xe_ref, res_ref, stage_ref, sem):
    p = pl.program_id(0)
    nt = pl.num_programs(0)
    par = p % 2
    P = TOKEN_TILE
    R = DISPATCH_CHUNK
    E = N_EXPERTS

    @pl.when(p == 0)
    def _():
        stage_ref[...] = jnp.zeros(stage_ref.shape, bf16)
        res_ref[...] = jnp.zeros(res_ref.shape, bf16)

    start = [st_ref[p * E + e] for e in range(E)]
    stop = [st_ref[(p + 1) * E + e] for e in range(E)]
    base = [_floor_tile(s) for s in start]
    done = [_floor_tile(t) - b for t, b in zip(stop, base)]
    partial = [t - _floor_tile(t) for t in stop]
    length = [d + jnp.where(r > 0, BF16_ROWS, 0) for d, r in zip(done, partial)]
    offs = _prefix(length)
    total = offs[E]

    def out_copy(buf, e, src, dst, n):
        n = pl.multiple_of(n, BF16_ROWS)
        return pltpu.make_async_copy(res_ref.at[buf, pl.ds(pl.multiple_of(src, BF16_ROWS), n), :],
                                     xe_ref.at[e, pl.ds(pl.multiple_of(dst, BF16_ROWS), n), :],
                                     sem.at[buf, e])

    slot = slot_ref[...]
    shift = _expert_vector([offs[e] - base[e] for e in range(E)], (E, P), 0)
    tgt_hi, tgt_lo = _split_bf16(jnp.where(slot >= 0, slot + shift, -1))
    pad = jnp.zeros((LANES - E, P), bf16)
    tgt_hi = jnp.concatenate([tgt_hi, pad], axis=0)
    tgt_lo = jnp.concatenate([tgt_lo, pad], axis=0)
    first = _expert_vector(offs[:E], (1, LANES), 1, fill=BIG)
    last = _expert_vector(offs[1:], (1, LANES), 1, fill=BIG)
    x = x_ref[...]

    def chunk(k, carry):
        row0 = pl.multiple_of(k * R, R)
        rid = row0 + lax.broadcasted_iota(i32, (R, LANES), 0)
        owner = ((rid >= first) & (rid < last)).astype(bf16)
        want = (jnp.dot(owner, tgt_hi, preferred_element_type=f32)
                + jnp.dot(owner, tgt_lo, preferred_element_type=f32))
        rid2 = (row0 + lax.broadcasted_iota(i32, (R, P), 0)).astype(f32)
        onehot = (want == rid2).astype(bf16)
        res_ref[par, pl.ds(row0, R), :] = jnp.dot(onehot, x, preferred_element_type=f32).astype(bf16)
        return carry

    lax.fori_loop(0, _ceil_div_pow2(total, R), chunk, 0)

    for e in range(E):
        head = pl.multiple_of(offs[e], BF16_ROWS)
        tail = pl.multiple_of(head + done[e], BF16_ROWS)
        srows = slice(e * BF16_ROWS, (e + 1) * BF16_ROWS)
        res_ref[par, pl.ds(head, BF16_ROWS), :] = (res_ref[par, pl.ds(head, BF16_ROWS), :]
                                                   + stage_ref[srows, :])
        stage_ref[srows, :] = jnp.where(partial[e] > 0, res_ref[par, pl.ds(tail, BF16_ROWS), :],
                                        jnp.zeros((BF16_ROWS, XAUG_COLS), bf16))

        @pl.when(done[e] > 0)
        def _(e=e, head=head):
            out_copy(par, e, head, base[e], done[e]).start()

    @pl.when(p > 0)
    def _():
        for e in range(E):
            n_prev = base[e] - _floor_tile(st_ref[(p - 1) * E + e])

            @pl.when(n_prev > 0)
            def _(e=e, n_prev=n_prev):
                out_copy(1 - par, e, 0, 0, n_prev).wait()

    @pl.when(p == nt - 1)
    def _():
        for e in range(E):
            @pl.when(done[e] > 0)
            def _(e=e):
                out_copy(par, e, 0, 0, done[e]).wait()


def _dispatch(xaug, slot, starts, cap):
    T = xaug.shape[0]
    P = TOKEN_TILE
    grid_spec = pltpu.PrefetchScalarGridSpec(
        num_scalar_prefetch=1,
        grid=(T // P,),
        in_specs=[pl.BlockSpec((P, XAUG_COLS), lambda i, st: (i, 0)),
                  pl.BlockSpec((N_EXPERTS, P), lambda i, st: (0, i))],
        out_specs=pl.BlockSpec(memory_space=pl.ANY),
        scratch_shapes=[pltpu.VMEM((2, STACK_ROWS, XAUG_COLS), bf16),
                        pltpu.VMEM((N_EXPERTS * BF16_ROWS, XAUG_COLS), bf16),
                        pltpu.SemaphoreType.DMA((2, N_EXPERTS))],
    )
    return pl.pallas_call(
        _dispatch_kernel,
        grid_spec=grid_spec,
        out_shape=jax.ShapeDtypeStruct((N_EXPERTS, cap, XAUG_COLS), bf16),
        compiler_params=_cparams(("arbitrary",)),
        name="dispatch",
    )(starts, xaug, slot)


FF_CHUNK = 512


def _ffn_kernel(xe_ref, wg_ref, wu_ref, wd_ref, ye_ref):
    e = pl.program_id(0)
    x = xe_ref[0, :, 0:D_MODEL]
    gsplit = xe_ref[0, :, D_MODEL:XAUG_COLS].astype(f32)
    lane = lax.broadcasted_iota(i32, gsplit.shape, 1)
    mine = (lane == e) | (lane == e + N_EXPERTS) | (lane == e + 2 * N_EXPERTS)
    gate = jnp.sum(jnp.where(mine, gsplit, 0.0), axis=-1, keepdims=True)
    acc = jnp.zeros((x.shape[0], D_MODEL), f32)
    for f in range(0, D_FF, FF_CHUNK):
        g = jnp.dot(x, wg_ref[0, 0, :, f:f + FF_CHUNK], preferred_element_type=f32)
        u = jnp.dot(x, wu_ref[0, 0, :, f:f + FF_CHUNK], preferred_element_type=f32)
        h = (g * _sigmoid(g) * u).astype(bf16)
        acc = acc + jnp.dot(h, wd_ref[0, 0, f:f + FF_CHUNK, :], preferred_element_type=f32)
    ye_ref[0] = (acc * gate).astype(bf16)


def _ffn(xe, wg, wu, wd, layer, cap):
    tm = min(PROJ_TILE, cap)
    wspec = lambda rows, cols: pl.BlockSpec((1, 1, rows, cols), lambda e, j: (layer, e, 0, 0))
    return pl.pallas_call(
        _ffn_kernel,
        grid=(N_EXPERTS, cap // tm),
        in_specs=[pl.BlockSpec((1, tm, XAUG_COLS), lambda e, j: (e, j, 0)),
                  wspec(D_MODEL, D_FF), wspec(D_MODEL, D_FF), wspec(D_FF, D_MODEL)],
        out_specs=pl.BlockSpec((1, tm, D_MODEL), lambda e, j: (e, j, 0)),
        out_shape=jax.ShapeDtypeStruct((N_EXPERTS, cap, D_MODEL), bf16),
        compiler_params=_cparams(("arbitrary", "arbitrary")),
        name="experts",
    )(xe, wg, wu, wd)


def _combine_kernel(st_ref, x1_ref, slott_ref, ye_ref, g_ref, o_ref, ybuf_ref, sem):
    p = pl.program_id(0)
    nt = pl.num_programs(0)
    par = p % 2
    P = TOKEN_TILE
    KC = COMBINE_CHUNK
    E = N_EXPERTS

    def geometry(q):
        start = [st_ref[q * E + e] for e in range(E)]
        stop = [st_ref[(q + 1) * E + e] for e in range(E)]
        base = [_floor_tile(s) for s in start]
        length = [jnp.where(t > s, _floor_tile(t + BF16_ROWS - 1) - b, 0)
                  for s, t, b in zip(start, stop, base)]
        return base, length, _prefix(length)

    def in_copy(buf, e, src, dst, n):
        n = pl.multiple_of(n, BF16_ROWS)
        return pltpu.make_async_copy(ye_ref.at[e, pl.ds(pl.multiple_of(src, BF16_ROWS), n), :],
                                     ybuf_ref.at[buf, pl.ds(pl.multiple_of(dst, BF16_ROWS), n), :],
                                     sem.at[buf, e])

    def fetch(q, buf):
        base, length, offs = geometry(q)
        for e in range(E):
            @pl.when(length[e] > 0)
            def _(e=e):
                in_copy(buf, e, base[e], offs[e], length[e]).start()

    @pl.when(p == 0)
    def _():
        ybuf_ref[...] = jnp.zeros(ybuf_ref.shape, bf16)
        fetch(0, 0)

    @pl.when(p + 1 < nt)
    def _():
        fetch(p + 1, 1 - par)

    base, length, offs = geometry(p)
    total = offs[E]
    slots = slott_ref[...]
    shift = _expert_vector([offs[e] - base[e] for e in range(E)], (1, LANES), 1)
    tgt_hi, tgt_lo = _split_bf16(jnp.where(slots >= 0, slots + shift, -1))
    first = _expert_vector(offs[:E], (E, KC), 0)
    last = _expert_vector(offs[1:], (E, KC), 0)
    pad = jnp.zeros((LANES - E, KC), bf16)
    for e in range(E):
        @pl.when(length[e] > 0)
        def _(e=e):
            in_copy(par, e, 0, 0, length[e]).wait()

    o_ref[...] = x1_ref[...]

    def chunk(k, carry):
        col0 = pl.multiple_of(k * KC, KC)
        cid = col0 + lax.broadcasted_iota(i32, (E, KC), 1)
        owner = jnp.concatenate([((cid >= first) & (cid < last)).astype(bf16), pad], axis=0)
        want = (jnp.dot(tgt_hi, owner, preferred_element_type=f32)
                + jnp.dot(tgt_lo, owner, preferred_element_type=f32))
        cid2 = (col0 + lax.broadcasted_iota(i32, (P, KC), 1)).astype(f32)
        onehot = (want == cid2).astype(bf16)
        o_ref[...] = o_ref[...] + jnp.dot(onehot, ybuf_ref[par, pl.ds(col0, KC), :],
                                          preferred_element_type=f32)
        return carry

    lax.fori_loop(0, _ceil_div_pow2(total, KC), chunk, 0)
    if g_ref is not None:
        x2 = o_ref[...]
        o_ref[...] = x2 * lax.rsqrt(jnp.mean(x2 * x2, axis=-1, keepdims=True) + EPS) * g_ref[...]


def _combine_body(st_ref, x1_ref, slott_ref, ye_ref, o_ref, ybuf_ref, sem):
    _combine_kernel(st_ref, x1_ref, slott_ref, ye_ref, None, o_ref, ybuf_ref, sem)


def _combine(x1, slott, ye, starts, final_g=None):
    T = x1.shape[0]
    P = TOKEN_TILE
    in_specs = [pl.BlockSpec((P, D_MODEL), lambda i, st: (i, 0)),
                pl.BlockSpec((P, LANES), lambda i, st: (i, 0)),
                pl.BlockSpec(memory_space=pl.ANY)]
    args = [starts, x1, slott, ye]
    body = _combine_body
    if final_g is not None:
        in_specs.append(pl.BlockSpec((1, D_MODEL), lambda i, st: (0, 0)))
        args.append(final_g)
        body = _combine_kernel
    grid_spec = pltpu.PrefetchScalarGridSpec(
        num_scalar_prefetch=1,
        grid=(T // P,),
        in_specs=in_specs,
        out_specs=pl.BlockSpec((P, D_MODEL), lambda i, st: (i, 0)),
        scratch_shapes=[pltpu.VMEM((2, STACK_ROWS, D_MODEL), bf16),
                        pltpu.SemaphoreType.DMA((2, N_EXPERTS))],
    )
    return pl.pallas_call(
        body,
        grid_spec=grid_spec,
        out_shape=jax.ShapeDtypeStruct((T, D_MODEL), f32),
        compiler_params=_cparams(("arbitrary",)),
        name="combine",
    )(*args)


def _rope_tables(seq):
    half = ROPE_DIMS // 2
    pos = jnp.arange(seq, dtype=f32)
    inv_freq = jnp.exp(-jnp.log(jnp.float32(ROPE_THETA)) * jnp.arange(half, dtype=f32) * (2.0 / ROPE_DIMS))
    ang = pos[:, None] * inv_freq[None, :]
    cos, sin = jnp.cos(ang), jnp.sin(ang)
    rest = HEAD_DIM - ROPE_DIMS
    ones = jnp.ones((seq, rest), f32)
    zeros_h = jnp.zeros((seq, half), f32)
    zeros_r = jnp.zeros((seq, rest), f32)
    c = jnp.concatenate([cos, cos, ones], axis=1)
    s1 = jnp.concatenate([zeros_h, sin, zeros_r], axis=1)
    s2 = jnp.concatenate([-sin, zeros_h, zeros_r], axis=1)
    rep = LANES // HEAD_DIM
    return jnp.tile(c, (1, rep)), jnp.tile(s1, (1, rep)), jnp.tile(s2, (1, rep))


def _per_lane(v):
    return jnp.repeat(v.astype(f32), HEAD_DIM)[None, :]


def _trunk(x, params):
    batch, seq, _ = x.shape
    T = batch * seq
    cap = CAPACITY_FACTOR * T // N_EXPERTS
    assert seq % (RET_CHUNK * RET_CHUNKS_PER_STEP) == 0 and seq % PROJ_TILE == 0
    assert seq % (BLOCK * ATTN_QBLOCKS) == 0 and T % (TOKEN_TILE * min(RANK_TILES, T // TOKEN_TILE)) == 0
    assert cap % BF16_ROWS == 0 and cap % min(PROJ_TILE, cap) == 0
    x = x.reshape(T, D_MODEL)
    rc, rs1, rs2 = _rope_tables(seq)
    depth = params["w_in"].shape[0]
    for l in range(depth):
        q, kv, ret = _inproj(x, params["norm1_g"][l][None, :], params["w_in"], l, rc, rs1, rs2, seq)
        sink_rows = jnp.broadcast_to(params["attn_sink"][l].astype(f32)[:, None], (N_ATTN_HEADS, LANES))
        a_out = _attention(q, kv, sink_rows, params["attn_out_g"][l][None, :], batch, seq)
        r_out = _retention(ret, _per_lane(params["ret_decay_fwd"][l]), _per_lane(params["ret_decay_bwd"][l]),
                           params["ret_out_g"][l][None, :], batch, seq)
        x1, xaug, afft = _outproj(a_out, r_out, x, params["w_out"], params["norm2_g"][l][None, :],
                                  params["w_router"], l)
        slot, slott, starts = _select(afft, cap)
        xe = _dispatch(xaug, slot, starts, cap)
        ye = _ffn(xe, params["w_gate"], params["w_up"], params["w_down"], l, cap)
        final_g = params["final_norm_g"][None, :] if l == depth - 1 else None
        x = _combine(x1, slott, ye, starts, final_g)
    return x.reshape(batch, seq, D_MODEL)


def kernel(x_prompt, x_sample, norm1_g, w_in, attn_sink, ret_decay_fwd, ret_decay_bwd, attn_out_g, ret_out_g,
           w_out, norm2_g, w_router, w_gate, w_up, w_down, final_norm_g):
    wr = jnp.pad(w_router, ((0, 0), (0, 0), (0, LANES - N_EXPERTS)))
    params = dict(
        norm1_g=norm1_g, w_in=w_in.astype(bf16), attn_sink=attn_sink, ret_decay_fwd=ret_decay_fwd,
        ret_decay_bwd=ret_decay_bwd, attn_out_g=attn_out_g, ret_out_g=ret_out_g, w_out=w_out.astype(bf16),
        norm2_g=norm2_g, w_router=wr.astype(bf16), w_gate=w_gate.astype(bf16), w_up=w_up.astype(bf16),
        w_down=w_down.astype(bf16), final_norm_g=final_norm_g)
    return (_trunk(x_prompt, params), _trunk(x_sample, params))
```

```python
import functools

import jax
import jax.numpy as jnp
from jax import lax
from jax.experimental import pallas as pl
from jax.experimental.pallas import tpu as pltpu

f32 = jnp.float32
bf16 = jnp.bfloat16
i32 = jnp.int32

D_MODEL = 1024
HEAD_DIM = 64
N_ATTN_HEADS = 8
N_KV_HEADS = 2
N_RET_HEADS = 8
ATTN_WIDTH = N_ATTN_HEADS * HEAD_DIM
KV_WIDTH = N_KV_HEADS * HEAD_DIM
RET_WIDTH = N_RET_HEADS * HEAD_DIM
QKV_COLS = ATTN_WIDTH + 2 * KV_WIDTH
IN_COLS = QKV_COLS + 4 * RET_WIDTH
WINDOW = 128
BLOCK = 128
RET_CHUNK = 128
ROPE_THETA = 500000.0
ROPE_DIMS = HEAD_DIM // 4
N_EXPERTS = 16
CAPACITY_FACTOR = 2
D_FF = 2 * D_MODEL
EPS = 1e-6
NEG = -1e30

LANES = 128
BF16_ROWS = 16
GATE_COLS = LANES
XAUG_COLS = D_MODEL + GATE_COLS
TOKEN_TILE = 256
DISPATCH_CHUNK = 512
COMBINE_CHUNK = 1024
STACK_ROWS = N_EXPERTS * (TOKEN_TILE + 2 * BF16_ROWS)
STACK_ROWS = -(-STACK_ROWS // COMBINE_CHUNK) * COMBINE_CHUNK
PROJ_TILE = 512
RANK_TILES = 4
ATTN_QBLOCKS = 4
RET_CHUNKS_PER_STEP = 4
VMEM_LIMIT = 48 * 1024 * 1024


def _cparams(sem):
    return pltpu.CompilerParams(dimension_semantics=sem, vmem_limit_bytes=VMEM_LIMIT)


def _sigmoid(x):
    return 1.0 / (1.0 + jnp.exp(-x))


def _log_sigmoid(x):
    return jnp.minimum(x, 0.0) - jnp.log(1.0 + jnp.exp(-jnp.abs(x)))


def _inproj_kernel(x_ref, g_ref, w_ref, c_ref, s1_ref, s2_ref, q_ref, kv_ref, ret_ref):
    x = x_ref[...]
    h = x * lax.rsqrt(jnp.mean(x * x, axis=-1, keepdims=True) + EPS) * g_ref[...]
    hb = h.astype(bf16)
    c, s1, s2 = c_ref[...], s1_ref[...], s2_ref[...]

    def rope(p):
        half = ROPE_DIMS // 2
        return p * c + pltpu.roll(p, half, 1) * s1 + pltpu.roll(p, LANES - half, 1) * s2

    qkv = jnp.dot(hb, w_ref[0, :, 0:QKV_COLS], preferred_element_type=f32)
    for j in range(ATTN_WIDTH // LANES):
        q_ref[:, j * LANES:(j + 1) * LANES] = rope(qkv[:, j * LANES:(j + 1) * LANES]).astype(bf16)
    kv_ref[:, 0:KV_WIDTH] = rope(qkv[:, ATTN_WIDTH:ATTN_WIDTH + KV_WIDTH]).astype(bf16)
    kv_ref[:, KV_WIDTH:2 * KV_WIDTH] = qkv[:, ATTN_WIDTH + KV_WIDTH:QKV_COLS].astype(bf16)
    for j in range(4):
        lo = QKV_COLS + j * RET_WIDTH
        ret_ref[:, j * RET_WIDTH:(j + 1) * RET_WIDTH] = jnp.dot(
            hb, w_ref[0, :, lo:lo + RET_WIDTH], preferred_element_type=f32).astype(bf16)


def _inproj(x, g, w, layer, rc, rs1, rs2, seq):
    T = x.shape[0]
    tm = PROJ_TILE
    npos = seq // tm
    row = lambda i: (i, 0)
    pos = lambda i: (i % npos, 0)
    fixed = lambda i: (0, 0)
    return pl.pallas_call(
        _inproj_kernel,
        grid=(T // tm,),
        in_specs=[pl.BlockSpec((tm, D_MODEL), row), pl.BlockSpec((1, D_MODEL), fixed),
                  pl.BlockSpec((1, D_MODEL, IN_COLS), lambda i: (layer, 0, 0)),
                  pl.BlockSpec((tm, LANES), pos), pl.BlockSpec((tm, LANES), pos),
                  pl.BlockSpec((tm, LANES), pos)],
        out_specs=[pl.BlockSpec((tm, ATTN_WIDTH), row), pl.BlockSpec((tm, 2 * KV_WIDTH), row),
                   pl.BlockSpec((tm, 4 * RET_WIDTH), row)],
        out_shape=[jax.ShapeDtypeStruct((T, ATTN_WIDTH), bf16),
                   jax.ShapeDtypeStruct((T, 2 * KV_WIDTH), bf16),
                   jax.ShapeDtypeStruct((T, 4 * RET_WIDTH), bf16)],
        compiler_params=_cparams(("arbitrary",)),
        name="inproj",
    )(x, g, w, rc, rs1, rs2)


def _attn_kernel(q_ref, kp_ref, kc_ref, kn_ref, sink_ref, g_ref, o_ref):
    n = pl.program_id(1)
    nsteps = pl.num_programs(1)
    QB = ATTN_QBLOCKS
    kv = jnp.concatenate([kp_ref[...], kc_ref[...], kn_ref[...]], axis=0)
    rows = kv.shape[0]
    lane = lax.broadcasted_iota(i32, (rows, LANES), 1)
    low = lane < HEAD_DIM
    kf = kv[:, 0:LANES].astype(f32) * (HEAD_DIM ** -0.5)
    kr = pltpu.roll(kf, HEAD_DIM, 1)
    k_low = [jnp.where(low, kf, 0.0).astype(bf16), jnp.where(low, kr, 0.0).astype(bf16)]
    k_high = [jnp.where(low, 0.0, kr).astype(bf16), jnp.where(low, 0.0, kf).astype(bf16)]
    vt = kv[:, LANES:2 * LANES].astype(f32).T
    vs = jnp.concatenate([vt[HEAD_DIM:], vt[:HEAD_DIM]], axis=0)
    sub = lax.broadcasted_iota(i32, (LANES, rows), 0)
    top = sub < HEAD_DIM
    ones_mid = (sub == HEAD_DIM).astype(f32)
    ones_top = (sub == 0).astype(f32)
    vt_low = [jnp.where(top, vt, ones_mid).astype(bf16), jnp.where(top, vs, ones_mid).astype(bf16)]
    vt_high = [jnp.where(top, ones_top, vs).astype(bf16), jnp.where(top, ones_top, vt).astype(bf16)]

    ki = lax.broadcasted_iota(i32, (3 * BLOCK, BLOCK), 0)
    qi = lax.broadcasted_iota(i32, (3 * BLOCK, BLOCK), 1)
    in_window = jnp.abs(ki - BLOCK - qi) <= WINDOW
    topq = lax.broadcasted_iota(i32, (LANES, BLOCK), 0) < HEAD_DIM
    grp = N_ATTN_HEADS // N_KV_HEADS
    g_row = g_ref[...]

    def scores(i):
        blk = n * QB + i
        valid = in_window & ((ki >= BLOCK) | (blk > 0)) & ((ki < 2 * BLOCK) | (blk < nsteps * QB - 1))
        bias = jnp.where(valid, 0.0, NEG)
        keys = slice(i * BLOCK, (i + 3) * BLOCK)
        out = []
        for h in range(N_ATTN_HEADS):
            qp = q_ref[i * BLOCK:(i + 1) * BLOCK, (h // 2) * LANES:(h // 2 + 1) * LANES]
            kx = (k_high if h % 2 else k_low)[h // grp]
            s = lax.dot_general(kx[keys], qp, (((1,), (1,)), ((), ())), preferred_element_type=f32)
            out.append(jnp.concatenate([s[0:BLOCK] + bias[0:BLOCK], s[BLOCK:2 * BLOCK],
                                        s[2 * BLOCK:] + bias[2 * BLOCK:]], axis=0))
        return out

    pending = scores(0)
    for i in range(QB):
        current = pending
        if i + 1 < QB:
            pending = scores(i + 1)
        keys = slice(i * BLOCK, (i + 3) * BLOCK)
        probs, maxes = [], []
        for h in range(N_ATTN_HEADS):
            sk = sink_ref[h:h + 1, 0:1]
            m = jnp.maximum(jnp.max(current[h], axis=0, keepdims=True), sk)
            probs.append(jnp.exp(current[h] - m).astype(bf16))
            maxes.append(m)
        halves = []
        for h in range(N_ATTN_HEADS):
            vx = (vt_high if h % 2 else vt_low)[h // grp]
            ones_at = 0 if h % 2 else HEAD_DIM
            oa = jnp.dot(vx[:, keys], probs[h], preferred_element_type=f32)
            denom = oa[ones_at:ones_at + 1, :] + jnp.exp(sink_ref[h:h + 1, 0:1] - maxes[h])
            halves.append(oa / denom)
        pieces = [jnp.where(topq, halves[2 * j], halves[2 * j + 1]).T for j in range(N_ATTN_HEADS // 2)]
        a = jnp.concatenate(pieces, axis=1)
        a = a * lax.rsqrt(jnp.mean(a * a, axis=-1, keepdims=True) + EPS) * g_row
        o_ref[i * BLOCK:(i + 1) * BLOCK, :] = a.astype(bf16)


def _attention(q, kv, sink_rows, g, batch, seq):
    T = q.shape[0]
    QB = ATTN_QBLOCKS
    nb = seq // BLOCK
    ns = nb // QB
    cur = lambda b, n: (b * ns + n, 0)
    prev = lambda b, n: (b * nb + jnp.maximum(n * QB - 1, 0), 0)
    nxt = lambda b, n: (b * nb + jnp.minimum(n * QB + QB, nb - 1), 0)
    fixed = lambda b, n: (0, 0)
    return pl.pallas_call(
        _attn_kernel,
        grid=(batch, ns),
        in_specs=[pl.BlockSpec((QB * BLOCK, ATTN_WIDTH), cur),
                  pl.BlockSpec((BLOCK, 2 * KV_WIDTH), prev),
                  pl.BlockSpec((QB * BLOCK, 2 * KV_WIDTH), cur),
                  pl.BlockSpec((BLOCK, 2 * KV_WIDTH), nxt),
                  pl.BlockSpec((N_ATTN_HEADS, LANES), fixed),
                  pl.BlockSpec((1, ATTN_WIDTH), fixed)],
        out_specs=pl.BlockSpec((QB * BLOCK, ATTN_WIDTH), cur),
        out_shape=jax.ShapeDtypeStruct((T, ATTN_WIDTH), bf16),
        compiler_params=_cparams(("arbitrary", "arbitrary")),
        name="attention",
    )(q, kv, kv, kv, sink_rows, g)


N_PAIRS = RET_WIDTH // LANES


def _pair_masks():
    r = lax.broadcasted_iota(i32, (LANES, LANES), 0)
    c = lax.broadcasted_iota(i32, (LANES, LANES), 1)
    low = c < HEAD_DIM
    blockdiag = (r < HEAD_DIM) == (c < HEAD_DIM)
    return low, blockdiag


def _ret_bstate_kernel(k_ref, v_ref, db_ref, sb_ref, st_ref):
    @pl.when(pl.program_id(1) == 0)
    def _():
        st_ref[...] = jnp.zeros(st_ref.shape, f32)

    C = RET_CHUNK
    lg = _log_sigmoid(db_ref[...])
    idx = lax.broadcasted_iota(i32, (C, RET_WIDTH), 0).astype(f32)
    zeta = jnp.exp(lg * idx)
    gc = jnp.exp(lg * float(C))
    _, blockdiag = _pair_masks()
    for c in reversed(range(RET_CHUNKS_PER_STEP)):
        rows = slice(c * C, (c + 1) * C)
        for j in range(N_PAIRS):
            lanes = slice(j * LANES, (j + 1) * LANES)
            kp = k_ref[rows, lanes].astype(f32) * (HEAD_DIM ** -0.5)
            vp = v_ref[rows, lanes]
            st = st_ref[j]
            sb_ref[c, j] = st.astype(bf16)
            kz = (kp * zeta[:, lanes]).T.astype(bf16)
            upd = jnp.dot(kz, vp, preferred_element_type=f32)
            st_ref[j] = jnp.where(blockdiag, gc[:, lanes] * st + upd, 0.0)


def _ret_main_kernel(q_ref, k_ref, v_ref, gate_ref, sb_ref, df_ref, db_ref, gn_ref, o_ref, st_ref):
    @pl.when(pl.program_id(1) == 0)
    def _():
        st_ref[...] = jnp.zeros(st_ref.shape, f32)

    C = RET_CHUNK
    lgf = _log_sigmoid(df_ref[...])
    lgb = _log_sigmoid(db_ref[...])
    idx = lax.broadcasted_iota(i32, (C, RET_WIDTH), 0).astype(f32)
    xi_f = jnp.exp(lgf * (idx + 1.0))
    xi_b = jnp.exp(lgb * (float(C) - idx))
    zeta_f = jnp.exp(lgf * (float(C - 1) - idx))
    gc_f = jnp.exp(lgf * float(C))
    low, blockdiag = _pair_masks()
    diff = (lax.broadcasted_iota(i32, (C, C), 0) - lax.broadcasted_iota(i32, (C, C), 1)).astype(f32)

    def head_decay(h):
        a = lgf[:, h * HEAD_DIM:h * HEAD_DIM + 1]
        b = lgb[:, h * HEAD_DIM:h * HEAD_DIM + 1]
        return jnp.where(diff >= 0.0, jnp.exp(a * jnp.maximum(diff, 0.0)),
                         jnp.exp(b * jnp.maximum(-diff, 0.0)))

    decay = [jnp.concatenate([head_decay(2 * j), head_decay(2 * j + 1)], axis=0) for j in range(N_PAIRS)]
    gn = gn_ref[...]
    inv_hd = 1.0 / HEAD_DIM

    for c in range(RET_CHUNKS_PER_STEP):
        rows = slice(c * C, (c + 1) * C)
        for j in range(N_PAIRS):
            lanes = slice(j * LANES, (j + 1) * LANES)
            qp = q_ref[rows, lanes]
            qf = qp.astype(f32)
            kp = k_ref[rows, lanes].astype(f32) * (HEAD_DIM ** -0.5)
            kb = kp.astype(bf16)
            vp = v_ref[rows, lanes]
            zero = jnp.zeros_like(qp)
            q2 = jnp.concatenate([jnp.where(low, qp, zero), jnp.where(low, zero, qp)], axis=0)
            s2 = lax.dot_general(q2, kb, (((1,), (1,)), ((), ())), preferred_element_type=f32)
            o2 = jnp.dot((s2 * decay[j]).astype(bf16), vp, preferred_element_type=f32)
            inner = jnp.where(low, o2[0:C], o2[C:2 * C])
            st = st_ref[j]
            cross_f = jnp.dot((qf * xi_f[:, lanes]).astype(bf16), st.astype(bf16),
                              preferred_element_type=f32)
            cross_b = jnp.dot((qf * xi_b[:, lanes]).astype(bf16), sb_ref[c, j],
                              preferred_element_type=f32)
            o = inner + cross_f + cross_b
            s_lo = jnp.sum(jnp.where(low, o, 0.0), axis=-1, keepdims=True)
            s_hi = jnp.sum(jnp.where(low, 0.0, o), axis=-1, keepdims=True)
            d = o - jnp.where(low, s_lo, s_hi) * inv_hd
            dd = d * d
            v_lo = jnp.sum(jnp.where(low, dd, 0.0), axis=-1, keepdims=True)
            v_hi = jnp.sum(jnp.where(low, 0.0, dd), axis=-1, keepdims=True)
            var = jnp.where(low, v_lo, v_hi) * inv_hd
            on = d * lax.rsqrt(var + EPS) * gn[:, lanes]
            gate = gate_ref[rows, lanes].astype(f32)
            o_ref[rows, lanes] = (gate * _sigmoid(gate) * on).astype(bf16)
            kz = (kp * zeta_f[:, lanes]).T.astype(bf16)
            upd = jnp.dot(kz, vp, preferred_element_type=f32)
            st_ref[j] = jnp.where(blockdiag, gc_f[:, lanes] * st + upd, 0.0)


def _retention(ret, dec_f, dec_b, gn, batch, seq):
    T = ret.shape[0]
    C = RET_CHUNK
    ch = RET_CHUNKS_PER_STEP
    nblk = seq // (C * ch)
    rows = ch * C
    fixed = lambda b, i: (0, 0)
    col = lambda j: (lambda b, i: (b * nblk + i, j))
    rcol = lambda j: (lambda b, i: (b * nblk + nblk - 1 - i, j))
    sb = pl.pallas_call(
        _ret_bstate_kernel,
        grid=(batch, nblk),
        in_specs=[pl.BlockSpec((rows, RET_WIDTH), rcol(1)), pl.BlockSpec((rows, RET_WIDTH), rcol(2)),
                  pl.BlockSpec((1, RET_WIDTH), fixed)],
        out_specs=pl.BlockSpec((ch, N_PAIRS, LANES, LANES), lambda b, i: (b * nblk + nblk - 1 - i, 0, 0, 0)),
        out_shape=jax.ShapeDtypeStruct((T // C, N_PAIRS, LANES, LANES), bf16),
        scratch_shapes=[pltpu.VMEM((N_PAIRS, LANES, LANES), f32)],
        compiler_params=_cparams(("arbitrary", "arbitrary")),
        name="ret_bstate",
    )(ret, ret, dec_b)
    return pl.pallas_call(
        _ret_main_kernel,
        grid=(batch, nblk),
        in_specs=[pl.BlockSpec((rows, RET_WIDTH), col(0)), pl.BlockSpec((rows, RET_WIDTH), col(1)),
                  pl.BlockSpec((rows, RET_WIDTH), col(2)), pl.BlockSpec((rows, RET_WIDTH), col(3)),
                  pl.BlockSpec((ch, N_PAIRS, LANES, LANES), lambda b, i: (b * nblk + i, 0, 0, 0)),
                  pl.BlockSpec((1, RET_WIDTH), fixed), pl.BlockSpec((1, RET_WIDTH), fixed),
                  pl.BlockSpec((1, RET_WIDTH), fixed)],
        out_specs=pl.BlockSpec((rows, RET_WIDTH), col(0)),
        out_shape=jax.ShapeDtypeStruct((T, RET_WIDTH), bf16),
        scratch_shapes=[pltpu.VMEM((N_PAIRS, LANES, LANES), f32)],
        compiler_params=_cparams(("arbitrary", "arbitrary")),
        name="ret_main",
    )(ret, ret, ret, ret, sb, dec_f, dec_b, gn)


def _outproj_kernel(a_ref, r_ref, x_ref, w_ref, g_ref, wrt_ref, x1_ref, xaug_ref, afft_ref):
    tm = x_ref.shape[0]
    parts = [slice(0, tm // 2), slice(tm // 2, tm)]
    w = w_ref[0]
    proj = [jnp.dot(jnp.concatenate([a_ref[s, :], r_ref[s, :]], axis=1), w, preferred_element_type=f32)
            for s in parts]
    hbs = []
    for s, y in zip(parts, proj):
        x1 = x_ref[s, :] + y
        x1_ref[s, :] = x1
        h2 = x1 * lax.rsqrt(jnp.mean(x1 * x1, axis=-1, keepdims=True) + EPS) * g_ref[...]
        hb = h2.astype(bf16)
        xaug_ref[s, 0:D_MODEL] = hb
        hbs.append(hb)
    logits = [lax.dot_general(wrt_ref[0], hb, (((1,), (1,)), ((), ())), preferred_element_type=f32)
              for hb in hbs]
    for s, lt in zip(parts, logits):
        lg = lt[0:N_EXPERTS, :]
        e = jnp.exp(lg - jnp.max(lg, axis=0, keepdims=True))
        aff = e / jnp.sum(e, axis=0, keepdims=True)
        afft_ref[:, s] = aff
        hi = aff.astype(bf16).astype(f32)
        r1 = aff - hi
        mid = r1.astype(bf16).astype(f32)
        lo = (r1 - mid).astype(bf16).astype(f32)
        rows = jnp.concatenate([hi, mid, lo, jnp.zeros((LANES - 3 * N_EXPERTS, aff.shape[1]), f32)], axis=0)
        xaug_ref[s, D_MODEL:XAUG_COLS] = rows.T.astype(bf16)


def _outproj(a, r, x, w, g, wr, layer):
    T = x.shape[0]
    tm = PROJ_TILE
    row = lambda i: (i, 0)
    fixed = lambda i: (0, 0)
    lay = lambda i: (layer, 0, 0)
    return pl.pallas_call(
        _outproj_kernel,
        grid=(T // tm,),
        in_specs=[pl.BlockSpec((tm, ATTN_WIDTH), row), pl.BlockSpec((tm, RET_WIDTH), row),
                  pl.BlockSpec((tm, D_MODEL), row), pl.BlockSpec((1, D_MODEL, D_MODEL), lay),
                  pl.BlockSpec((1, D_MODEL), fixed), pl.BlockSpec((1, LANES, D_MODEL), lay)],
        out_specs=[pl.BlockSpec((tm, D_MODEL), row), pl.BlockSpec((tm, XAUG_COLS), row),
                   pl.BlockSpec((N_EXPERTS, tm), lambda i: (0, i))],
        out_shape=[jax.ShapeDtypeStruct((T, D_MODEL), f32), jax.ShapeDtypeStruct((T, XAUG_COLS), bf16),
                   jax.ShapeDtypeStruct((N_EXPERTS, T), f32)],
        compiler_params=_cparams(("arbitrary",)),
        name="outproj_router",
    )(a, r, x, w, g, wr)


SCAN_LANES = 512


def _thresh_kernel(a_ref, thr_ref, need_ref, *, cap):
    T = a_ref.shape[1]
    nchunk = T // SCAN_LANES

    def count_ge(cand):
        def body(c, acc):
            off = pl.multiple_of(c * SCAN_LANES, SCAN_LANES)
            keys = lax.bitcast_convert_type(a_ref[:, pl.ds(off, SCAN_LANES)], i32)
            return acc + (keys >= cand).astype(i32)
        acc = lax.fori_loop(0, nchunk, body, jnp.zeros((N_EXPERTS, SCAN_LANES), i32))
        return jnp.sum(acc, axis=1, keepdims=True)

    def bit_body(i, prefix):
        cand = prefix | (jnp.int32(1) << (30 - i))
        return jnp.where(count_ge(cand) >= cap, cand, prefix)

    thr = lax.fori_loop(0, 31, bit_body, jnp.zeros((N_EXPERTS, 1), i32))
    need = cap - count_ge(thr + 1)
    thr_ref[...] = jnp.broadcast_to(thr, thr_ref.shape)
    need_ref[...] = jnp.broadcast_to(need, need_ref.shape)


def _slots_kernel(a_ref, thr_ref, need_ref, slot_ref, slott_ref, start_ref, ceq_ref, csel_ref):
    @pl.when(pl.program_id(0) == 0)
    def _():
        ceq_ref[...] = jnp.zeros(ceq_ref.shape, f32)
        csel_ref[...] = jnp.zeros(csel_ref.shape, f32)

    P = TOKEN_TILE
    thr = thr_ref[:, 0:1]
    need = need_ref[:, 0:1].astype(f32)
    before = (lax.broadcasted_iota(i32, (P, P), 0) < lax.broadcasted_iota(i32, (P, P), 1)).astype(bf16)
    ceq = ceq_ref[:, 0:1]
    csel = csel_ref[:, 0:1]
    for j in range(a_ref.shape[1] // P):
        cols = slice(j * P, (j + 1) * P)
        keys = lax.bitcast_convert_type(a_ref[:, cols], i32)
        gt = keys > thr
        eq = keys == thr
        eqf = eq.astype(f32)
        eq_rank = jnp.dot(eqf.astype(bf16), before, preferred_element_type=f32) + ceq
        sel = gt | (eq & (eq_rank < need))
        self_ = sel.astype(f32)
        rank = jnp.dot(self_.astype(bf16), before, preferred_element_type=f32) + csel
        slot = jnp.where(sel, rank, -1.0)
        slot_ref[:, cols] = slot.astype(i32)
        padded = jnp.concatenate([slot, jnp.full((LANES - N_EXPERTS, P), -1.0, f32)], axis=0)
        slott_ref[cols, :] = padded.T.astype(i32)
        start_ref[j] = jnp.broadcast_to(csel, (N_EXPERTS, LANES)).astype(i32)
        ceq = ceq + jnp.sum(eqf, axis=1, keepdims=True)
        csel = csel + jnp.sum(self_, axis=1, keepdims=True)
    ceq_ref[...] = jnp.broadcast_to(ceq, ceq_ref.shape)
    csel_ref[...] = jnp.broadcast_to(csel, csel_ref.shape)


def _select(afft, cap):
    T = afft.shape[1]
    P = TOKEN_TILE
    nt = T // P
    rt = min(RANK_TILES, nt)
    thr, need = pl.pallas_call(
        functools.partial(_thresh_kernel, cap=cap),
        out_shape=[jax.ShapeDtypeStruct((N_EXPERTS, LANES), i32)] * 2,
        compiler_params=pltpu.CompilerParams(vmem_limit_bytes=VMEM_LIMIT),
        name="topk_threshold",
    )(afft)
    fixed = lambda i: (0, 0)
    slot, slott, starts = pl.pallas_call(
        _slots_kernel,
        grid=(nt // rt,),
        in_specs=[pl.BlockSpec((N_EXPERTS, rt * P), lambda i: (0, i)),
                  pl.BlockSpec((N_EXPERTS, LANES), fixed), pl.BlockSpec((N_EXPERTS, LANES), fixed)],
        out_specs=[pl.BlockSpec((N_EXPERTS, rt * P), lambda i: (0, i)),
                   pl.BlockSpec((rt * P, LANES), lambda i: (i, 0)),
                   pl.BlockSpec((rt, N_EXPERTS, LANES), lambda i: (i, 0, 0))],
        out_shape=[jax.ShapeDtypeStruct((N_EXPERTS, T), i32), jax.ShapeDtypeStruct((T, LANES), i32),
                   jax.ShapeDtypeStruct((nt, N_EXPERTS, LANES), i32)],
        scratch_shapes=[pltpu.VMEM((N_EXPERTS, LANES), f32), pltpu.VMEM((N_EXPERTS, LANES), f32)],
        compiler_params=_cparams(("arbitrary",)),
        name="slot_ranks",
    )(afft, thr, need)
    starts = jnp.concatenate([starts[:, :, 0], jnp.full((1, N_EXPERTS), cap, i32)], axis=0)
    return slot, slott, starts.reshape(-1)


BIG = 1 << 30


ROW_SHIFT = BF16_ROWS.bit_length() - 1


def _floor_tile(v):
    return (v >> ROW_SHIFT) << ROW_SHIFT


def _ceil_div_pow2(v, d):
    return (v + d - 1) >> (d.bit_length() - 1)


def _expert_vector(vals, shape, axis, fill=0):
    idx = lax.broadcasted_iota(i32, shape, axis)
    out = jnp.full(shape, fill, i32)
    for e, v in enumerate(vals):
        out = jnp.where(idx == e, v, out)
    return out


def _prefix(lengths):
    offs = [jnp.int32(0)]
    for n in lengths:
        offs.append(offs[-1] + n)
    return offs


def _split_bf16(tgt):
    hi = jnp.where(tgt >= 0, (tgt >> 6) << 6, 0)
    lo = jnp.where(tgt >= 0, tgt & 63, -1)
    return hi.astype(f32).astype(bf16), lo.astype(f32).astype(bf16)


def _dispatch_kernel(st_ref, x_ref, slot_ref, xe_ref, res_ref, stage_ref, sem):
    p = pl.program_id(0)
    nt = pl.num_programs(0)
    par = p % 2
    P = TOKEN_TILE
    R = DISPATCH_CHUNK
    E = N_EXPERTS

    @pl.when(p == 0)
    def _():
        stage_ref[...] = jnp.zeros(stage_ref.shape, bf16)
        res_ref[...] = jnp.zeros(res_ref.shape, bf16)

    start = [st_ref[p * E + e] for e in range(E)]
    stop = [st_ref[(p + 1) * E + e] for e in range(E)]
    base = [_floor_tile(s) for s in start]
    done = [_floor_tile(t) - b for t, b in zip(stop, base)]
    partial = [t - _floor_tile(t) for t in stop]
    length = [d + jnp.where(r > 0, BF16_ROWS, 0) for d, r in zip(done, partial)]
    offs = _prefix(length)
    total = offs[E]

    def out_copy(buf, e, src, dst, n):
        n = pl.multiple_of(n, BF16_ROWS)
        return pltpu.make_async_copy(res_ref.at[buf, pl.ds(pl.multiple_of(src, BF16_ROWS), n), :],
                                     xe_ref.at[e, pl.ds(pl.multiple_of(dst, BF16_ROWS), n), :],
                                     sem.at[buf, e])

    slot = slot_ref[...]
    shift = _expert_vector([offs[e] - base[e] for e in range(E)], (E, P), 0)
    tgt_hi, tgt_lo = _split_bf16(jnp.where(slot >= 0, slot + shift, -1))
    pad = jnp.zeros((LANES - E, P), bf16)
    tgt_hi = jnp.concatenate([tgt_hi, pad], axis=0)
    tgt_lo = jnp.concatenate([tgt_lo, pad], axis=0)
    first = _expert_vector(offs[:E], (1, LANES), 1, fill=BIG)
    last = _expert_vector(offs[1:], (1, LANES), 1, fill=BIG)
    x = x_ref[...]

    def chunk(k, carry):
        row0 = pl.multiple_of(k * R, R)
        rid = row0 + lax.broadcasted_iota(i32, (R, LANES), 0)
        owner = ((rid >= first) & (rid < last)).astype(bf16)
        want = (jnp.dot(owner, tgt_hi, preferred_element_type=f32)
                + jnp.dot(owner, tgt_lo, preferred_element_type=f32))
        rid2 = (row0 + lax.broadcasted_iota(i32, (R, P), 0)).astype(f32)
        onehot = (want == rid2).astype(bf16)
        res_ref[par, pl.ds(row0, R), :] = jnp.dot(onehot, x, preferred_element_type=f32).astype(bf16)
        return carry

    lax.fori_loop(0, _ceil_div_pow2(total, R), chunk, 0)

    for e in range(E):
        head = pl.multiple_of(offs[e], BF16_ROWS)
        tail = pl.multiple_of(head + done[e], BF16_ROWS)
        srows = slice(e * BF16_ROWS, (e + 1) * BF16_ROWS)
        res_ref[par, pl.ds(head, BF16_ROWS), :] = (res_ref[par, pl.ds(head, BF16_ROWS), :]
                                                   + stage_ref[srows, :])
        stage_ref[srows, :] = jnp.where(partial[e] > 0, res_ref[par, pl.ds(tail, BF16_ROWS), :],
                                        jnp.zeros((BF16_ROWS, XAUG_COLS), bf16))

        @pl.when(done[e] > 0)
        def _(e=e, head=head):
            out_copy(par, e, head, base[e], done[e]).start()

    @pl.when(p > 0)
    def _():
        for e in range(E):
            n_prev = base[e] - _floor_tile(st_ref[(p - 1) * E + e])

            @pl.when(n_prev > 0)
            def _(e=e, n_prev=n_prev):
                out_copy(1 - par, e, 0, 0, n_prev).wait()

    @pl.when(p == nt - 1)
    def _():
        for e in range(E):
            @pl.when(done[e] > 0)
            def _(e=e):
                out_copy(par, e, 0, 0, done[e]).wait()


def _dispatch(xaug, slot, starts, cap):
    T = xaug.shape[0]
    P = TOKEN_TILE
    grid_spec = pltpu.PrefetchScalarGridSpec(
        num_scalar_prefetch=1,
        grid=(T // P,),
        in_specs=[pl.BlockSpec((P, XAUG_COLS), lambda i, st: (i, 0)),
                  pl.BlockSpec((N_EXPERTS, P), lambda i, st: (0, i))],
        out_specs=pl.BlockSpec(memory_space=pl.ANY),
        scratch_shapes=[pltpu.VMEM((2, STACK_ROWS, XAUG_COLS), bf16),
                        pltpu.VMEM((N_EXPERTS * BF16_ROWS, XAUG_COLS), bf16),
                        pltpu.SemaphoreType.DMA((2, N_EXPERTS))],
    )
    return pl.pallas_call(
        _dispatch_kernel,
        grid_spec=grid_spec,
        out_shape=jax.ShapeDtypeStruct((N_EXPERTS, cap, XAUG_COLS), bf16),
        compiler_params=_cparams(("arbitrary",)),
        name="dispatch",
    )(starts, xaug, slot)


FF_CHUNK = 512


def _ffn_kernel(xe_ref, wg_ref, wu_ref, wd_ref, ye_ref):
    e = pl.program_id(0)
    x = xe_ref[0, :, 0:D_MODEL]
    gsplit = xe_ref[0, :, D_MODEL:XAUG_COLS].astype(f32)
    lane = lax.broadcasted_iota(i32, gsplit.shape, 1)
    mine = (lane == e) | (lane == e + N_EXPERTS) | (lane == e + 2 * N_EXPERTS)
    gate = jnp.sum(jnp.where(mine, gsplit, 0.0), axis=-1, keepdims=True)
    acc = jnp.zeros((x.shape[0], D_MODEL), f32)
    for f in range(0, D_FF, FF_CHUNK):
        g = jnp.dot(x, wg_ref[0, 0, :, f:f + FF_CHUNK], preferred_element_type=f32)
        u = jnp.dot(x, wu_ref[0, 0, :, f:f + FF_CHUNK], preferred_element_type=f32)
        h = (g * _sigmoid(g) * u).astype(bf16)
        acc = acc + jnp.dot(h, wd_ref[0, 0, f:f + FF_CHUNK, :], preferred_element_type=f32)
    ye_ref[0] = (acc * gate).astype(bf16)


def _ffn(xe, wg, wu, wd, layer, cap):
    tm = min(PROJ_TILE, cap)
    wspec = lambda rows, cols: pl.BlockSpec((1, 1, rows, cols), lambda e, j: (layer, e, 0, 0))
    return pl.pallas_call(
        _ffn_kernel,
        grid=(N_EXPERTS, cap // tm),
        in_specs=[pl.BlockSpec((1, tm, XAUG_COLS), lambda e, j: (e, j, 0)),
                  wspec(D_MODEL, D_FF), wspec(D_MODEL, D_FF), wspec(D_FF, D_MODEL)],
        out_specs=pl.BlockSpec((1, tm, D_MODEL), lambda e, j: (e, j, 0)),
        out_shape=jax.ShapeDtypeStruct((N_EXPERTS, cap, D_MODEL), bf16),
        compiler_params=_cparams(("arbitrary", "arbitrary")),
        name="experts",
    )(xe, wg, wu, wd)


def _combine_kernel(st_ref, x1_ref, slott_ref, ye_ref, g_ref, o_ref, ybuf_ref, sem):
    p = pl.program_id(0)
    nt = pl.num_programs(0)
    par = p % 2
    P = TOKEN_TILE
    KC = COMBINE_CHUNK
    E = N_EXPERTS

    def geometry(q):
        start = [st_ref[q * E + e] for e in range(E)]
        stop = [st_ref[(q + 1) * E + e] for e in range(E)]
        base = [_floor_tile(s) for s in start]
        length = [jnp.where(t > s, _floor_tile(t + BF16_ROWS - 1) - b, 0)
                  for s, t, b in zip(start, stop, base)]
        return base, length, _prefix(length)

    def in_copy(buf, e, src, dst, n):
        n = pl.multiple_of(n, BF16_ROWS)
        return pltpu.make_async_copy(ye_ref.at[e, pl.ds(pl.multiple_of(src, BF16_ROWS), n), :],
                                     ybuf_ref.at[buf, pl.ds(pl.multiple_of(dst, BF16_ROWS), n), :],
                                     sem.at[buf, e])

    def fetch(q, buf):
        base, length, offs = geometry(q)
        for e in range(E):
            @pl.when(length[e] > 0)
            def _(e=e):
                in_copy(buf, e, base[e], offs[e], length[e]).start()

    @pl.when(p == 0)
    def _():
        ybuf_ref[...] = jnp.zeros(ybuf_ref.shape, bf16)
        fetch(0, 0)

    @pl.when(p + 1 < nt)
    def _():
        fetch(p + 1, 1 - par)

    base, length, offs = geometry(p)
    total = offs[E]
    slots = slott_ref[...]
    shift = _expert_vector([offs[e] - base[e] for e in range(E)], (1, LANES), 1)
    tgt_hi, tgt_lo = _split_bf16(jnp.where(slots >= 0, slots + shift, -1))
    first = _expert_vector(offs[:E], (E, KC), 0)
    last = _expert_vector(offs[1:], (E, KC), 0)
    pad = jnp.zeros((LANES - E, KC), bf16)
    for e in range(E):
        @pl.when(length[e] > 0)
        def _(e=e):
            in_copy(par, e, 0, 0, length[e]).wait()

    o_ref[...] = x1_ref[...]

    def chunk(k, carry):
        col0 = pl.multiple_of(k * KC, KC)
        cid = col0 + lax.broadcasted_iota(i32, (E, KC), 1)
        owner = jnp.concatenate([((cid >= first) & (cid < last)).astype(bf16), pad], axis=0)
        want = (jnp.dot(tgt_hi, owner, preferred_element_type=f32)
                + jnp.dot(tgt_lo, owner, preferred_element_type=f32))
        cid2 = (col0 + lax.broadcasted_iota(i32, (P, KC), 1)).astype(f32)
        onehot = (want == cid2).astype(bf16)
        o_ref[...] = o_ref[...] + jnp.dot(onehot, ybuf_ref[par, pl.ds(col0, KC), :],
                                          preferred_element_type=f32)
        return carry

    lax.fori_loop(0, _ceil_div_pow2(total, KC), chunk, 0)
    if g_ref is not None:
        x2 = o_ref[...]
        o_ref[...] = x2 * lax.rsqrt(jnp.mean(x2 * x2, axis=-1, keepdims=True) + EPS) * g_ref[...]


def _combine_body(st_ref, x1_ref, slott_ref, ye_ref, o_ref, ybuf_ref, sem):
    _combine_kernel(st_ref, x1_ref, slott_ref, ye_ref, None, o_ref, ybuf_ref, sem)


def _combine(x1, slott, ye, starts, final_g=None):
    T = x1.shape[0]
    P = TOKEN_TILE
    in_specs = [pl.BlockSpec((P, D_MODEL), lambda i, st: (i, 0)),
                pl.BlockSpec((P, LANES), lambda i, st: (i, 0)),
                pl.BlockSpec(memory_space=pl.ANY)]
    args = [starts, x1, slott, ye]
    body = _combine_body
    if final_g is not None:
        in_specs.append(pl.BlockSpec((1, D_MODEL), lambda i, st: (0, 0)))
        args.append(final_g)
        body = _combine_kernel
    grid_spec = pltpu.PrefetchScalarGridSpec(
        num_scalar_prefetch=1,
        grid=(T // P,),
        in_specs=in_specs,
        out_specs=pl.BlockSpec((P, D_MODEL), lambda i, st: (i, 0)),
        scratch_shapes=[pltpu.VMEM((2, STACK_ROWS, D_MODEL), bf16),
                        pltpu.SemaphoreType.DMA((2, N_EXPERTS))],
    )
    return pl.pallas_call(
        body,
        grid_spec=grid_spec,
        out_shape=jax.ShapeDtypeStruct((T, D_MODEL), f32),
        compiler_params=_cparams(("arbitrary",)),
        name="combine",
    )(*args)


def _rope_tables(seq):
    half = ROPE_DIMS // 2
    pos = jnp.arange(seq, dtype=f32)
    inv_freq = jnp.exp(-jnp.log(jnp.float32(ROPE_THETA)) * jnp.arange(half, dtype=f32) * (2.0 / ROPE_DIMS))
    ang = pos[:, None] * inv_freq[None, :]
    cos, sin = jnp.cos(ang), jnp.sin(ang)
    rest = HEAD_DIM - ROPE_DIMS
    ones = jnp.ones((seq, rest), f32)
    zeros_h = jnp.zeros((seq, half), f32)
    zeros_r = jnp.zeros((seq, rest), f32)
    c = jnp.concatenate([cos, cos, ones], axis=1)
    s1 = jnp.concatenate([zeros_h, sin, zeros_r], axis=1)
    s2 = jnp.concatenate([-sin, zeros_h, zeros_r], axis=1)
    rep = LANES // HEAD_DIM
    return jnp.tile(c, (1, rep)), jnp.tile(s1, (1, rep)), jnp.tile(s2, (1, rep))


def _per_lane(v):
    return jnp.repeat(v.astype(f32), HEAD_DIM)[None, :]


def _trunk(x, params):
    batch, seq, _ = x.shape
    T = batch * seq
    cap = CAPACITY_FACTOR * T // N_EXPERTS
    assert seq % (RET_CHUNK * RET_CHUNKS_PER_STEP) == 0 and seq % PROJ_TILE == 0
    assert seq % (BLOCK * ATTN_QBLOCKS) == 0 and T % (TOKEN_TILE * min(RANK_TILES, T // TOKEN_TILE)) == 0
    assert cap % BF16_ROWS == 0 and cap % min(PROJ_TILE, cap) == 0
    x = x.reshape(T, D_MODEL)
    rc, rs1, rs2 = _rope_tables(seq)
    depth = params["w_in"].shape[0]
    for l in range(depth):
        q, kv, ret = _inproj(x, params["norm1_g"][l][None, :], params["w_in"], l, rc, rs1, rs2, seq)
        sink_rows = jnp.broadcast_to(params["attn_sink"][l].astype(f32)[:, None], (N_ATTN_HEADS, LANES))
        a_out = _attention(q, kv, sink_rows, params["attn_out_g"][l][None, :], batch, seq)
        r_out = _retention(ret, _per_lane(params["ret_decay_fwd"][l]), _per_lane(params["ret_decay_bwd"][l]),
                           params["ret_out_g"][l][None, :], batch, seq)
        x1, xaug, afft = _outproj(a_out, r_out, x, params["w_out"], params["norm2_g"][l][None, :],
                                  params["w_router"], l)
        slot, slott, starts = _select(afft, cap)
        xe = _dispatch(xaug, slot, starts, cap)
        ye = _ffn(xe, params["w_gate"], params["w_up"], params["w_down"], l, cap)
        final_g = params["final_norm_g"][None, :] if l == depth - 1 else None
        x = _combine(x1, slott, ye, starts, final_g)
    return x.reshape(batch, seq, D_MODEL)


def kernel(x_prompt, x_sample, norm1_g, w_in, attn_sink, ret_decay_fwd, ret_decay_bwd, attn_out_g, ret_out_g,
           w_out, norm2_g, w_router, w_gate, w_up, w_down, final_norm_g):
    wr = jnp.pad(jnp.swapaxes(w_router, 1, 2), ((0, 0), (0, LANES - N_EXPERTS), (0, 0)))
    params = dict(
        norm1_g=norm1_g, w_in=w_in.astype(bf16), attn_sink=attn_sink, ret_decay_fwd=ret_decay_fwd,
        ret_decay_bwd=ret_decay_bwd, attn_out_g=attn_out_g, ret_out_g=ret_out_g, w_out=w_out.astype(bf16),
        norm2_g=norm2_g, w_router=wr.astype(bf16), w_gate=w_gate.astype(bf16), w_up=w_up.astype(bf16),
        w_down=w_down.astype(bf16), final_norm_g=final_norm_g)
    return (_trunk(x_prompt, params), _trunk(x_sample, params))
```

```python
import functools

import jax
import jax.numpy as jnp
from jax import lax
from jax.experimental import pallas as pl
from jax.experimental.pallas import tpu as pltpu

f32 = jnp.float32
bf16 = jnp.bfloat16
i32 = jnp.int32

D_MODEL = 1024
HEAD_DIM = 64
N_ATTN_HEADS = 8
N_KV_HEADS = 2
N_RET_HEADS = 8
ATTN_WIDTH = N_ATTN_HEADS * HEAD_DIM
KV_WIDTH = N_KV_HEADS * HEAD_DIM
RET_WIDTH = N_RET_HEADS * HEAD_DIM
QKV_COLS = ATTN_WIDTH + 2 * KV_WIDTH
IN_COLS = QKV_COLS + 4 * RET_WIDTH
WINDOW = 128
BLOCK = 128
RET_CHUNK = 128
ROPE_THETA = 500000.0
ROPE_DIMS = HEAD_DIM // 4
N_EXPERTS = 16
CAPACITY_FACTOR = 2
D_FF = 2 * D_MODEL
EPS = 1e-6
NEG = -1e30

LANES = 128
BF16_ROWS = 16
GATE_COLS = LANES
XAUG_COLS = D_MODEL + GATE_COLS
TOKEN_TILE = 256
DISPATCH_CHUNK = 512
COMBINE_CHUNK = 1024
STACK_ROWS = N_EXPERTS * (TOKEN_TILE + 2 * BF16_ROWS)
STACK_ROWS = -(-STACK_ROWS // COMBINE_CHUNK) * COMBINE_CHUNK
PROJ_TILE = 512
RANK_TILES = 16
ATTN_QBLOCKS = 4
RET_CHUNKS_PER_STEP = 8
VMEM_LIMIT = 48 * 1024 * 1024


def _cparams(sem):
    return pltpu.CompilerParams(dimension_semantics=sem, vmem_limit_bytes=VMEM_LIMIT)


def _sigmoid(x):
    return 1.0 / (1.0 + jnp.exp(-x))


def _log_sigmoid(x):
    return jnp.minimum(x, 0.0) - jnp.log(1.0 + jnp.exp(-jnp.abs(x)))


def _inproj_kernel(x_ref, g_ref, w_ref, c_ref, s1_ref, s2_ref, q_ref, kv_ref, ret_ref):
    x = x_ref[...]
    h = x * lax.rsqrt(jnp.mean(x * x, axis=-1, keepdims=True) + EPS) * g_ref[...]
    hb = h.astype(bf16)
    c, s1, s2 = c_ref[...], s1_ref[...], s2_ref[...]

    def rope(p):
        half = ROPE_DIMS // 2
        return p * c + pltpu.roll(p, half, 1) * s1 + pltpu.roll(p, LANES - half, 1) * s2

    qkv = jnp.dot(hb, w_ref[0, :, 0:QKV_COLS], preferred_element_type=f32)
    for j in range(ATTN_WIDTH // LANES):
        q_ref[:, j * LANES:(j + 1) * LANES] = rope(qkv[:, j * LANES:(j + 1) * LANES]).astype(bf16)
    kv_ref[:, 0:KV_WIDTH] = rope(qkv[:, ATTN_WIDTH:ATTN_WIDTH + KV_WIDTH]).astype(bf16)
    kv_ref[:, KV_WIDTH:2 * KV_WIDTH] = qkv[:, ATTN_WIDTH + KV_WIDTH:QKV_COLS].astype(bf16)
    for j in range(4):
        lo = QKV_COLS + j * RET_WIDTH
        ret_ref[:, j * RET_WIDTH:(j + 1) * RET_WIDTH] = jnp.dot(
            hb, w_ref[0, :, lo:lo + RET_WIDTH], preferred_element_type=f32).astype(bf16)


def _inproj(x, g, w, layer, rc, rs1, rs2, seq):
    T = x.shape[0]
    tm = PROJ_TILE
    npos = seq // tm
    row = lambda i: (i, 0)
    pos = lambda i: (i % npos, 0)
    fixed = lambda i: (0, 0)
    return pl.pallas_call(
        _inproj_kernel,
        grid=(T // tm,),
        in_specs=[pl.BlockSpec((tm, D_MODEL), row), pl.BlockSpec((1, D_MODEL), fixed),
                  pl.BlockSpec((1, D_MODEL, IN_COLS), lambda i: (layer, 0, 0)),
                  pl.BlockSpec((tm, LANES), pos), pl.BlockSpec((tm, LANES), pos),
                  pl.BlockSpec((tm, LANES), pos)],
        out_specs=[pl.BlockSpec((tm, ATTN_WIDTH), row), pl.BlockSpec((tm, 2 * KV_WIDTH), row),
                   pl.BlockSpec((tm, 4 * RET_WIDTH), row)],
        out_shape=[jax.ShapeDtypeStruct((T, ATTN_WIDTH), bf16),
                   jax.ShapeDtypeStruct((T, 2 * KV_WIDTH), bf16),
                   jax.ShapeDtypeStruct((T, 4 * RET_WIDTH), bf16)],
        compiler_params=_cparams(("arbitrary",)),
        name="inproj",
    )(x, g, w, rc, rs1, rs2)


def _attn_kernel(q_ref, kp_ref, kc_ref, kn_ref, sink_ref, g_ref, o_ref):
    n = pl.program_id(1)
    nsteps = pl.num_programs(1)
    QB = ATTN_QBLOCKS
    kv = jnp.concatenate([kp_ref[...], kc_ref[...], kn_ref[...]], axis=0)
    rows = kv.shape[0]
    lane = lax.broadcasted_iota(i32, (rows, LANES), 1)
    low = lane < HEAD_DIM
    kf = kv[:, 0:LANES].astype(f32) * (HEAD_DIM ** -0.5)
    kr = pltpu.roll(kf, HEAD_DIM, 1)
    k_low = [jnp.where(low, kf, 0.0).astype(bf16), jnp.where(low, kr, 0.0).astype(bf16)]
    k_high = [jnp.where(low, 0.0, kr).astype(bf16), jnp.where(low, 0.0, kf).astype(bf16)]
    vt = kv[:, LANES:2 * LANES].astype(f32).T
    vs = jnp.concatenate([vt[HEAD_DIM:], vt[:HEAD_DIM]], axis=0)
    sub = lax.broadcasted_iota(i32, (LANES, rows), 0)
    top = sub < HEAD_DIM
    ones_mid = (sub == HEAD_DIM).astype(f32)
    ones_top = (sub == 0).astype(f32)
    vt_low = [jnp.where(top, vt, ones_mid).astype(bf16), jnp.where(top, vs, ones_mid).astype(bf16)]
    vt_high = [jnp.where(top, ones_top, vs).astype(bf16), jnp.where(top, ones_top, vt).astype(bf16)]

    ki = lax.broadcasted_iota(i32, (3 * BLOCK, BLOCK), 0)
    qi = lax.broadcasted_iota(i32, (3 * BLOCK, BLOCK), 1)
    in_window = jnp.abs(ki - BLOCK - qi) <= WINDOW
    topq = lax.broadcasted_iota(i32, (LANES, BLOCK), 0) < HEAD_DIM
    grp = N_ATTN_HEADS // N_KV_HEADS
    g_row = g_ref[...]

    def scores(i):
        blk = n * QB + i
        valid = in_window & ((ki >= BLOCK) | (blk > 0)) & ((ki < 2 * BLOCK) | (blk < nsteps * QB - 1))
        bias = jnp.where(valid, 0.0, NEG)
        keys = slice(i * BLOCK, (i + 3) * BLOCK)
        out = []
        for h in range(N_ATTN_HEADS):
            qp = q_ref[i * BLOCK:(i + 1) * BLOCK, (h // 2) * LANES:(h // 2 + 1) * LANES]
            kx = (k_high if h % 2 else k_low)[h // grp]
            s = lax.dot_general(kx[keys], qp, (((1,), (1,)), ((), ())), preferred_element_type=f32)
            out.append(jnp.concatenate([s[0:BLOCK] + bias[0:BLOCK], s[BLOCK:2 * BLOCK],
                                        s[2 * BLOCK:] + bias[2 * BLOCK:]], axis=0))
        return out

    pending = scores(0)
    for i in range(QB):
        current = pending
        if i + 1 < QB:
            pending = scores(i + 1)
        keys = slice(i * BLOCK, (i + 3) * BLOCK)
        probs, maxes = [], []
        for h in range(N_ATTN_HEADS):
            sk = sink_ref[h:h + 1, 0:1]
            m = jnp.maximum(jnp.max(current[h], axis=0, keepdims=True), sk)
            probs.append(jnp.exp(current[h] - m).astype(bf16))
            maxes.append(m)
        halves = []
        for h in range(N_ATTN_HEADS):
            vx = (vt_high if h % 2 else vt_low)[h // grp]
            ones_at = 0 if h % 2 else HEAD_DIM
            oa = jnp.dot(vx[:, keys], probs[h], preferred_element_type=f32)
            denom = oa[ones_at:ones_at + 1, :] + jnp.exp(sink_ref[h:h + 1, 0:1] - maxes[h])
            halves.append(oa / denom)
        pieces = [jnp.where(topq, halves[2 * j], halves[2 * j + 1]).T for j in range(N_ATTN_HEADS // 2)]
        a = jnp.concatenate(pieces, axis=1)
        a = a * lax.rsqrt(jnp.mean(a * a, axis=-1, keepdims=True) + EPS) * g_row
        o_ref[i * BLOCK:(i + 1) * BLOCK, :] = a.astype(bf16)


def _attention(q, kv, sink_rows, g, batch, seq):
    T = q.shape[0]
    QB = ATTN_QBLOCKS
    nb = seq // BLOCK
    ns = nb // QB
    cur = lambda b, n: (b * ns + n, 0)
    prev = lambda b, n: (b * nb + jnp.maximum(n * QB - 1, 0), 0)
    nxt = lambda b, n: (b * nb + jnp.minimum(n * QB + QB, nb - 1), 0)
    fixed = lambda b, n: (0, 0)
    return pl.pallas_call(
        _attn_kernel,
        grid=(batch, ns),
        in_specs=[pl.BlockSpec((QB * BLOCK, ATTN_WIDTH), cur),
                  pl.BlockSpec((BLOCK, 2 * KV_WIDTH), prev),
                  pl.BlockSpec((QB * BLOCK, 2 * KV_WIDTH), cur),
                  pl.BlockSpec((BLOCK, 2 * KV_WIDTH), nxt),
                  pl.BlockSpec((N_ATTN_HEADS, LANES), fixed),
                  pl.BlockSpec((1, ATTN_WIDTH), fixed)],
        out_specs=pl.BlockSpec((QB * BLOCK, ATTN_WIDTH), cur),
        out_shape=jax.ShapeDtypeStruct((T, ATTN_WIDTH), bf16),
        compiler_params=_cparams(("arbitrary", "arbitrary")),
        name="attention",
    )(q, kv, kv, kv, sink_rows, g)


N_PAIRS = RET_WIDTH // LANES


def _pair_masks():
    r = lax.broadcasted_iota(i32, (LANES, LANES), 0)
    c = lax.broadcasted_iota(i32, (LANES, LANES), 1)
    low = c < HEAD_DIM
    blockdiag = (r < HEAD_DIM) == (c < HEAD_DIM)
    return low, blockdiag


def _ret_bstate_kernel(k_ref, v_ref, db_ref, sb_ref, st_ref):
    @pl.when(pl.program_id(1) == 0)
    def _():
        st_ref[...] = jnp.zeros(st_ref.shape, f32)

    C = RET_CHUNK
    lg = _log_sigmoid(db_ref[...])
    idx = lax.broadcasted_iota(i32, (C, RET_WIDTH), 0).astype(f32)
    zeta = jnp.exp(lg * idx)
    gc = jnp.exp(lg * float(C))
    _, blockdiag = _pair_masks()
    for c in reversed(range(RET_CHUNKS_PER_STEP)):
        rows = slice(c * C, (c + 1) * C)
        for j in range(N_PAIRS):
            lanes = slice(j * LANES, (j + 1) * LANES)
            kp = k_ref[rows, lanes].astype(f32) * (HEAD_DIM ** -0.5)
            vp = v_ref[rows, lanes]
            st = st_ref[j]
            sb_ref[c, j] = st.astype(bf16)
            kz = (kp * zeta[:, lanes]).T.astype(bf16)
            upd = jnp.dot(kz, vp, preferred_element_type=f32)
            st_ref[j] = jnp.where(blockdiag, gc[:, lanes] * st + upd, 0.0)


def _ret_main_kernel(q_ref, k_ref, v_ref, gate_ref, sb_ref, df_ref, db_ref, gn_ref, o_ref, st_ref):
    @pl.when(pl.program_id(1) == 0)
    def _():
        st_ref[...] = jnp.zeros(st_ref.shape, f32)

    C = RET_CHUNK
    lgf = _log_sigmoid(df_ref[...])
    lgb = _log_sigmoid(db_ref[...])
    idx = lax.broadcasted_iota(i32, (C, RET_WIDTH), 0).astype(f32)
    xi_f = jnp.exp(lgf * (idx + 1.0))
    xi_b = jnp.exp(lgb * (float(C) - idx))
    zeta_f = jnp.exp(lgf * (float(C - 1) - idx))
    gc_f = jnp.exp(lgf * float(C))
    low, blockdiag = _pair_masks()
    diff = (lax.broadcasted_iota(i32, (C, C), 0) - lax.broadcasted_iota(i32, (C, C), 1)).astype(f32)

    def head_decay(h):
        a = lgf[:, h * HEAD_DIM:h * HEAD_DIM + 1]
        b = lgb[:, h * HEAD_DIM:h * HEAD_DIM + 1]
        return jnp.where(diff >= 0.0, jnp.exp(a * jnp.maximum(diff, 0.0)),
                         jnp.exp(b * jnp.maximum(-diff, 0.0)))

    decay = [jnp.concatenate([head_decay(2 * j), head_decay(2 * j + 1)], axis=0) for j in range(N_PAIRS)]
    gn = gn_ref[...]
    inv_hd = 1.0 / HEAD_DIM

    for c in range(RET_CHUNKS_PER_STEP):
        rows = slice(c * C, (c + 1) * C)
        for j in range(N_PAIRS):
            lanes = slice(j * LANES, (j + 1) * LANES)
            qp = q_ref[rows, lanes]
            qf = qp.astype(f32)
            kp = k_ref[rows, lanes].astype(f32) * (HEAD_DIM ** -0.5)
            kb = kp.astype(bf16)
            vp = v_ref[rows, lanes]
            zero = jnp.zeros_like(qp)
            q2 = jnp.concatenate([jnp.where(low, qp, zero), jnp.where(low, zero, qp)], axis=0)
            s2 = lax.dot_general(q2, kb, (((1,), (1,)), ((), ())), preferred_element_type=f32)
            o2 = jnp.dot((s2 * decay[j]).astype(bf16), vp, preferred_element_type=f32)
            inner = jnp.where(low, o2[0:C], o2[C:2 * C])
            st = st_ref[j]
            cross_f = jnp.dot((qf * xi_f[:, lanes]).astype(bf16), st.astype(bf16),
                              preferred_element_type=f32)
            cross_b = jnp.dot((qf * xi_b[:, lanes]).astype(bf16), sb_ref[c, j],
                              preferred_element_type=f32)
            o = inner + cross_f + cross_b
            s_lo = jnp.sum(jnp.where(low, o, 0.0), axis=-1, keepdims=True)
            s_hi = jnp.sum(jnp.where(low, 0.0, o), axis=-1, keepdims=True)
            d = o - jnp.where(low, s_lo, s_hi) * inv_hd
            dd = d * d
            v_lo = jnp.sum(jnp.where(low, dd, 0.0), axis=-1, keepdims=True)
            v_hi = jnp.sum(jnp.where(low, 0.0, dd), axis=-1, keepdims=True)
            var = jnp.where(low, v_lo, v_hi) * inv_hd
            on = d * lax.rsqrt(var + EPS) * gn[:, lanes]
            gate = gate_ref[rows, lanes].astype(f32)
            o_ref[rows, lanes] = (gate * _sigmoid(gate) * on).astype(bf16)
            kz = (kp * zeta_f[:, lanes]).T.astype(bf16)
            upd = jnp.dot(kz, vp, preferred_element_type=f32)
            st_ref[j] = jnp.where(blockdiag, gc_f[:, lanes] * st + upd, 0.0)


def _retention(ret, dec_f, dec_b, gn, batch, seq):
    T = ret.shape[0]
    C = RET_CHUNK
    ch = RET_CHUNKS_PER_STEP
    nblk = seq // (C * ch)
    rows = ch * C
    fixed = lambda b, i: (0, 0)
    col = lambda j: (lambda b, i: (b * nblk + i, j))
    rcol = lambda j: (lambda b, i: (b * nblk + nblk - 1 - i, j))
    sb = pl.pallas_call(
        _ret_bstate_kernel,
        grid=(batch, nblk),
        in_specs=[pl.BlockSpec((rows, RET_WIDTH), rcol(1)), pl.BlockSpec((rows, RET_WIDTH), rcol(2)),
                  pl.BlockSpec((1, RET_WIDTH), fixed)],
        out_specs=pl.BlockSpec((ch, N_PAIRS, LANES, LANES), lambda b, i: (b * nblk + nblk - 1 - i, 0, 0, 0)),
        out_shape=jax.ShapeDtypeStruct((T // C, N_PAIRS, LANES, LANES), bf16),
        scratch_shapes=[pltpu.VMEM((N_PAIRS, LANES, LANES), f32)],
        compiler_params=_cparams(("arbitrary", "arbitrary")),
        name="ret_bstate",
    )(ret, ret, dec_b)
    return pl.pallas_call(
        _ret_main_kernel,
        grid=(batch, nblk),
        in_specs=[pl.BlockSpec((rows, RET_WIDTH), col(0)), pl.BlockSpec((rows, RET_WIDTH), col(1)),
                  pl.BlockSpec((rows, RET_WIDTH), col(2)), pl.BlockSpec((rows, RET_WIDTH), col(3)),
                  pl.BlockSpec((ch, N_PAIRS, LANES, LANES), lambda b, i: (b * nblk + i, 0, 0, 0)),
                  pl.BlockSpec((1, RET_WIDTH), fixed), pl.BlockSpec((1, RET_WIDTH), fixed),
                  pl.BlockSpec((1, RET_WIDTH), fixed)],
        out_specs=pl.BlockSpec((rows, RET_WIDTH), col(0)),
        out_shape=jax.ShapeDtypeStruct((T, RET_WIDTH), bf16),
        scratch_shapes=[pltpu.VMEM((N_PAIRS, LANES, LANES), f32)],
        compiler_params=_cparams(("arbitrary", "arbitrary")),
        name="ret_main",
    )(ret, ret, ret, ret, sb, dec_f, dec_b, gn)


def _outproj_kernel(a_ref, r_ref, x_ref, w_ref, g_ref, wrt_ref, x1_ref, xaug_ref, afft_ref):
    tm = x_ref.shape[0]
    parts = [slice(0, tm // 2), slice(tm // 2, tm)]
    w = w_ref[0]
    proj = [jnp.dot(jnp.concatenate([a_ref[s, :], r_ref[s, :]], axis=1), w, preferred_element_type=f32)
            for s in parts]
    hbs = []
    for s, y in zip(parts, proj):
        x1 = x_ref[s, :] + y
        x1_ref[s, :] = x1
        h2 = x1 * lax.rsqrt(jnp.mean(x1 * x1, axis=-1, keepdims=True) + EPS) * g_ref[...]
        hb = h2.astype(bf16)
        xaug_ref[s, 0:D_MODEL] = hb
        hbs.append(hb)
    logits = [lax.dot_general(wrt_ref[0], hb, (((1,), (1,)), ((), ())), preferred_element_type=f32)
              for hb in hbs]
    for s, lt in zip(parts, logits):
        lg = lt[0:N_EXPERTS, :]
        e = jnp.exp(lg - jnp.max(lg, axis=0, keepdims=True))
        aff = e / jnp.sum(e, axis=0, keepdims=True)
        afft_ref[:, s] = aff
        hi = aff.astype(bf16).astype(f32)
        r1 = aff - hi
        mid = r1.astype(bf16).astype(f32)
        lo = (r1 - mid).astype(bf16).astype(f32)
        rows = jnp.concatenate([hi, mid, lo, jnp.zeros((LANES - 3 * N_EXPERTS, aff.shape[1]), f32)], axis=0)
        xaug_ref[s, D_MODEL:XAUG_COLS] = rows.T.astype(bf16)


def _outproj(a, r, x, w, g, wr, layer):
    T = x.shape[0]
    tm = PROJ_TILE
    row = lambda i: (i, 0)
    fixed = lambda i: (0, 0)
    lay = lambda i: (layer, 0, 0)
    return pl.pallas_call(
        _outproj_kernel,
        grid=(T // tm,),
        in_specs=[pl.BlockSpec((tm, ATTN_WIDTH), row), pl.BlockSpec((tm, RET_WIDTH), row),
                  pl.BlockSpec((tm, D_MODEL), row), pl.BlockSpec((1, D_MODEL, D_MODEL), lay),
                  pl.BlockSpec((1, D_MODEL), fixed), pl.BlockSpec((1, LANES, D_MODEL), lay)],
        out_specs=[pl.BlockSpec((tm, D_MODEL), row), pl.BlockSpec((tm, XAUG_COLS), row),
                   pl.BlockSpec((N_EXPERTS, tm), lambda i: (0, i))],
        out_shape=[jax.ShapeDtypeStruct((T, D_MODEL), f32), jax.ShapeDtypeStruct((T, XAUG_COLS), bf16),
                   jax.ShapeDtypeStruct((N_EXPERTS, T), f32)],
        compiler_params=_cparams(("arbitrary",)),
        name="outproj_router",
    )(a, r, x, w, g, wr)


SCAN_LANES = 512


def _thresh_kernel(a_ref, thr_ref, need_ref, *, cap):
    T = a_ref.shape[1]
    nchunk = T // SCAN_LANES

    def count_ge(cand):
        def body(c, acc):
            off = pl.multiple_of(c * SCAN_LANES, SCAN_LANES)
            keys = lax.bitcast_convert_type(a_ref[:, pl.ds(off, SCAN_LANES)], i32)
            return acc + (keys >= cand).astype(i32)
        acc = lax.fori_loop(0, nchunk, body, jnp.zeros((N_EXPERTS, SCAN_LANES), i32))
        return jnp.sum(acc, axis=1, keepdims=True)

    def bit_body(i, prefix):
        cand = prefix | (jnp.int32(1) << (30 - i))
        return jnp.where(count_ge(cand) >= cap, cand, prefix)

    thr = lax.fori_loop(0, 31, bit_body, jnp.zeros((N_EXPERTS, 1), i32))
    need = cap - count_ge(thr + 1)
    thr_ref[...] = jnp.broadcast_to(thr, thr_ref.shape)
    need_ref[...] = jnp.broadcast_to(need, need_ref.shape)


def _slots_kernel(a_ref, thr_ref, need_ref, slot_ref, slott_ref, start_ref, ceq_ref, csel_ref):
    @pl.when(pl.program_id(0) == 0)
    def _():
        ceq_ref[...] = jnp.zeros(ceq_ref.shape, f32)
        csel_ref[...] = jnp.zeros(csel_ref.shape, f32)

    P = TOKEN_TILE
    thr = thr_ref[:, 0:1]
    need = need_ref[:, 0:1].astype(f32)
    before = (lax.broadcasted_iota(i32, (P, P), 0) < lax.broadcasted_iota(i32, (P, P), 1)).astype(bf16)
    ceq = ceq_ref[:, 0:1]
    csel = csel_ref[:, 0:1]
    for j in range(a_ref.shape[1] // P):
        cols = slice(j * P, (j + 1) * P)
        keys = lax.bitcast_convert_type(a_ref[:, cols], i32)
        gt = keys > thr
        eq = keys == thr
        eqf = eq.astype(f32)
        eq_rank = jnp.dot(eqf.astype(bf16), before, preferred_element_type=f32) + ceq
        sel = gt | (eq & (eq_rank < need))
        self_ = sel.astype(f32)
        rank = jnp.dot(self_.astype(bf16), before, preferred_element_type=f32) + csel
        slot = jnp.where(sel, rank, -1.0)
        slot_ref[:, cols] = slot.astype(i32)
        padded = jnp.concatenate([slot, jnp.full((LANES - N_EXPERTS, P), -1.0, f32)], axis=0)
        slott_ref[cols, :] = padded.T.astype(i32)
        start_ref[j] = jnp.broadcast_to(csel, (N_EXPERTS, LANES)).astype(i32)
        ceq = ceq + jnp.sum(eqf, axis=1, keepdims=True)
        csel = csel + jnp.sum(self_, axis=1, keepdims=True)
    ceq_ref[...] = jnp.broadcast_to(ceq, ceq_ref.shape)
    csel_ref[...] = jnp.broadcast_to(csel, csel_ref.shape)


def _select(afft, cap):
    T = afft.shape[1]
    P = TOKEN_TILE
    nt = T // P
    rt = min(RANK_TILES, nt)
    thr, need = pl.pallas_call(
        functools.partial(_thresh_kernel, cap=cap),
        out_shape=[jax.ShapeDtypeStruct((N_EXPERTS, LANES), i32)] * 2,
        compiler_params=pltpu.CompilerParams(vmem_limit_bytes=VMEM_LIMIT),
        name="topk_threshold",
    )(afft)
    fixed = lambda i: (0, 0)
    slot, slott, starts = pl.pallas_call(
        _slots_kernel,
        grid=(nt // rt,),
        in_specs=[pl.BlockSpec((N_EXPERTS, rt * P), lambda i: (0, i)),
                  pl.BlockSpec((N_EXPERTS, LANES), fixed), pl.BlockSpec((N_EXPERTS, LANES), fixed)],
        out_specs=[pl.BlockSpec((N_EXPERTS, rt * P), lambda i: (0, i)),
                   pl.BlockSpec((rt * P, LANES), lambda i: (i, 0)),
                   pl.BlockSpec((rt, N_EXPERTS, LANES), lambda i: (i, 0, 0))],
        out_shape=[jax.ShapeDtypeStruct((N_EXPERTS, T), i32), jax.ShapeDtypeStruct((T, LANES), i32),
                   jax.ShapeDtypeStruct((nt, N_EXPERTS, LANES), i32)],
        scratch_shapes=[pltpu.VMEM((N_EXPERTS, LANES), f32), pltpu.VMEM((N_EXPERTS, LANES), f32)],
        compiler_params=_cparams(("arbitrary",)),
        name="slot_ranks",
    )(afft, thr, need)
    starts = jnp.concatenate([starts[:, :, 0], jnp.full((1, N_EXPERTS), cap, i32)], axis=0)
    return slot, slott, starts.reshape(-1)


BIG = 1 << 30


ROW_SHIFT = BF16_ROWS.bit_length() - 1


def _floor_tile(v):
    return (v >> ROW_SHIFT) << ROW_SHIFT


def _ceil_div_pow2(v, d):
    return (v + d - 1) >> (d.bit_length() - 1)


def _expert_vector(vals, shape, axis, fill=0):
    idx = lax.broadcasted_iota(i32, shape, axis)
    out = jnp.full(shape, fill, i32)
    for e, v in enumerate(vals):
        out = jnp.where(idx == e, v, out)
    return out


def _prefix(lengths):
    offs = [jnp.int32(0)]
    for n in lengths:
        offs.append(offs[-1] + n)
    return offs


def _split_bf16(tgt):
    hi = jnp.where(tgt >= 0, (tgt >> 6) << 6, 0)
    lo = jnp.where(tgt >= 0, tgt & 63, -1)
    return hi.astype(f32).astype(bf16), lo.astype(f32).astype(bf16)


def _dispatch_kernel(st_ref, x_ref, slot_ref, xe_ref, res_ref, stage_ref, sem):
    p = pl.program_id(0)
    nt = pl.num_programs(0)
    par = p % 2
    P = TOKEN_TILE
    R = DISPATCH_CHUNK
    E = N_EXPERTS

    @pl.when(p == 0)
    def _():
        stage_ref[...] = jnp.zeros(stage_ref.shape, bf16)
        res_ref[...] = jnp.zeros(res_ref.shape, bf16)

    start = [st_ref[p * E + e] for e in range(E)]
    stop = [st_ref[(p + 1) * E + e] for e in range(E)]
    base = [_floor_tile(s) for s in start]
    done = [_floor_tile(t) - b for t, b in zip(stop, base)]
    partial = [t - _floor_tile(t) for t in stop]
    length = [d + jnp.where(r > 0, BF16_ROWS, 0) for d, r in zip(done, partial)]
    offs = _prefix(length)
    total = offs[E]

    def out_copy(buf, e, src, dst, n):
        n = pl.multiple_of(n, BF16_ROWS)
        return pltpu.make_async_copy(res_ref.at[buf, pl.ds(pl.multiple_of(src, BF16_ROWS), n), :],
                                     xe_ref.at[e, pl.ds(pl.multiple_of(dst, BF16_ROWS), n), :],
                                     sem.at[buf, e])

    slot = slot_ref[...]
    shift = _expert_vector([offs[e] - base[e] for e in range(E)], (E, P), 0)
    tgt_hi, tgt_lo = _split_bf16(jnp.where(slot >= 0, slot + shift, -1))
    pad = jnp.zeros((LANES - E, P), bf16)
    tgt_parts = jnp.concatenate([tgt_hi, pad, tgt_lo, pad], axis=0)
    first = _expert_vector(offs[:E], (1, LANES), 1, fill=BIG)
    last = _expert_vector(offs[1:], (1, LANES), 1, fill=BIG)
    x = x_ref[...]

    def chunk(k, carry):
        row0 = pl.multiple_of(k * R, R)
        rid = row0 + lax.broadcasted_iota(i32, (R, LANES), 0)
        owner = ((rid >= first) & (rid < last)).astype(bf16)
        want = jnp.dot(jnp.concatenate([owner, owner], axis=1), tgt_parts,
                       preferred_element_type=f32)
        rid2 = (row0 + lax.broadcasted_iota(i32, (R, P), 0)).astype(f32)
        onehot = (want == rid2).astype(bf16)
        res_ref[par, pl.ds(row0, R), :] = jnp.dot(onehot, x, preferred_element_type=f32).astype(bf16)
        return carry

    lax.fori_loop(0, _ceil_div_pow2(total, R), chunk, 0)

    for e in range(E):
        head = pl.multiple_of(offs[e], BF16_ROWS)
        tail = pl.multiple_of(head + done[e], BF16_ROWS)
        srows = slice(e * BF16_ROWS, (e + 1) * BF16_ROWS)
        res_ref[par, pl.ds(head, BF16_ROWS), :] = (res_ref[par, pl.ds(head, BF16_ROWS), :]
                                                   + stage_ref[srows, :])
        stage_ref[srows, :] = jnp.where(partial[e] > 0, res_ref[par, pl.ds(tail, BF16_ROWS), :],
                                        jnp.zeros((BF16_ROWS, XAUG_COLS), bf16))

        @pl.when(done[e] > 0)
        def _(e=e, head=head):
            out_copy(par, e, head, base[e], done[e]).start()

    @pl.when(p > 0)
    def _():
        for e in range(E):
            n_prev = base[e] - _floor_tile(st_ref[(p - 1) * E + e])

            @pl.when(n_prev > 0)
            def _(e=e, n_prev=n_prev):
                out_copy(1 - par, e, 0, 0, n_prev).wait()

    @pl.when(p == nt - 1)
    def _():
        for e in range(E):
            @pl.when(done[e] > 0)
            def _(e=e):
                out_copy(par, e, 0, 0, done[e]).wait()


def _dispatch(xaug, slot, starts, cap):
    T = xaug.shape[0]
    P = TOKEN_TILE
    grid_spec = pltpu.PrefetchScalarGridSpec(
        num_scalar_prefetch=1,
        grid=(T // P,),
        in_specs=[pl.BlockSpec((P, XAUG_COLS), lambda i, st: (i, 0)),
                  pl.BlockSpec((N_EXPERTS, P), lambda i, st: (0, i))],
        out_specs=pl.BlockSpec(memory_space=pl.ANY),
        scratch_shapes=[pltpu.VMEM((2, STACK_ROWS, XAUG_COLS), bf16),
                        pltpu.VMEM((N_EXPERTS * BF16_ROWS, XAUG_COLS), bf16),
                        pltpu.SemaphoreType.DMA((2, N_EXPERTS))],
    )
    return pl.pallas_call(
        _dispatch_kernel,
        grid_spec=grid_spec,
        out_shape=jax.ShapeDtypeStruct((N_EXPERTS, cap, XAUG_COLS), bf16),
        compiler_params=_cparams(("arbitrary",)),
        name="dispatch",
    )(starts, xaug, slot)


FF_CHUNK = 512


def _ffn_kernel(xe_ref, wg_ref, wu_ref, wd_ref, ye_ref):
    e = pl.program_id(0)
    x = xe_ref[0, :, 0:D_MODEL]
    gsplit = xe_ref[0, :, D_MODEL:XAUG_COLS].astype(f32)
    lane = lax.broadcasted_iota(i32, gsplit.shape, 1)
    mine = (lane == e) | (lane == e + N_EXPERTS) | (lane == e + 2 * N_EXPERTS)
    gate = jnp.sum(jnp.where(mine, gsplit, 0.0), axis=-1, keepdims=True)
    acc = jnp.zeros((x.shape[0], D_MODEL), f32)
    for f in range(0, D_FF, FF_CHUNK):
        g = jnp.dot(x, wg_ref[0, 0, :, f:f + FF_CHUNK], preferred_element_type=f32)
        u = jnp.dot(x, wu_ref[0, 0, :, f:f + FF_CHUNK], preferred_element_type=f32)
        h = (g * _sigmoid(g) * u).astype(bf16)
        acc = acc + jnp.dot(h, wd_ref[0, 0, f:f + FF_CHUNK, :], preferred_element_type=f32)
    ye_ref[0] = (acc * gate).astype(bf16)


def _ffn(xe, wg, wu, wd, layer, cap):
    tm = min(PROJ_TILE, cap)
    wspec = lambda rows, cols: pl.BlockSpec((1, 1, rows, cols), lambda e, j: (layer, e, 0, 0))
    return pl.pallas_call(
        _ffn_kernel,
        grid=(N_EXPERTS, cap // tm),
        in_specs=[pl.BlockSpec((1, tm, XAUG_COLS), lambda e, j: (e, j, 0)),
                  wspec(D_MODEL, D_FF), wspec(D_MODEL, D_FF), wspec(D_FF, D_MODEL)],
        out_specs=pl.BlockSpec((1, tm, D_MODEL), lambda e, j: (e, j, 0)),
        out_shape=jax.ShapeDtypeStruct((N_EXPERTS, cap, D_MODEL), bf16),
        compiler_params=_cparams(("arbitrary", "arbitrary")),
        name="experts",
    )(xe, wg, wu, wd)


def _combine_kernel(st_ref, x1_ref, slott_ref, ye_ref, g_ref, o_ref, ybuf_ref, sem):
    p = pl.program_id(0)
    nt = pl.num_programs(0)
    par = p % 2
    P = TOKEN_TILE
    KC = COMBINE_CHUNK
    E = N_EXPERTS

    def geometry(q):
        start = [st_ref[q * E + e] for e in range(E)]
        stop = [st_ref[(q + 1) * E + e] for e in range(E)]
        base = [_floor_tile(s) for s in start]
        length = [jnp.where(t > s, _floor_tile(t + BF16_ROWS - 1) - b, 0)
                  for s, t, b in zip(start, stop, base)]
        return base, length, _prefix(length)

    def in_copy(buf, e, src, dst, n):
        n = pl.multiple_of(n, BF16_ROWS)
        return pltpu.make_async_copy(ye_ref.at[e, pl.ds(pl.multiple_of(src, BF16_ROWS), n), :],
                                     ybuf_ref.at[buf, pl.ds(pl.multiple_of(dst, BF16_ROWS), n), :],
                                     sem.at[buf, e])

    def fetch(q, buf):
        base, length, offs = geometry(q)
        for e in range(E):
            @pl.when(length[e] > 0)
            def _(e=e):
                in_copy(buf, e, base[e], offs[e], length[e]).start()

    @pl.when(p == 0)
    def _():
        ybuf_ref[...] = jnp.zeros(ybuf_ref.shape, bf16)
        fetch(0, 0)

    @pl.when(p + 1 < nt)
    def _():
        fetch(p + 1, 1 - par)

    base, length, offs = geometry(p)
    total = offs[E]
    slots = slott_ref[...]
    shift = _expert_vector([offs[e] - base[e] for e in range(E)], (1, LANES), 1)
    tgt_hi, tgt_lo = _split_bf16(jnp.where(slots >= 0, slots + shift, -1))
    tgt_parts = jnp.concatenate([tgt_hi, tgt_lo], axis=1)
    first = _expert_vector(offs[:E], (E, KC), 0)
    last = _expert_vector(offs[1:], (E, KC), 0)
    pad = jnp.zeros((LANES - E, KC), bf16)
    for e in range(E):
        @pl.when(length[e] > 0)
        def _(e=e):
            in_copy(par, e, 0, 0, length[e]).wait()

    o_ref[...] = x1_ref[...]

    def chunk(k, carry):
        col0 = pl.multiple_of(k * KC, KC)
        cid = col0 + lax.broadcasted_iota(i32, (E, KC), 1)
        own = ((cid >= first) & (cid < last)).astype(bf16)
        owner = jnp.concatenate([own, pad, own, pad], axis=0)
        want = jnp.dot(tgt_parts, owner, preferred_element_type=f32)
        cid2 = (col0 + lax.broadcasted_iota(i32, (P, KC), 1)).astype(f32)
        onehot = (want == cid2).astype(bf16)
        o_ref[...] = o_ref[...] + jnp.dot(onehot, ybuf_ref[par, pl.ds(col0, KC), :],
                                          preferred_element_type=f32)
        return carry

    lax.fori_loop(0, _ceil_div_pow2(total, KC), chunk, 0)
    if g_ref is not None:
        x2 = o_ref[...]
        o_ref[...] = x2 * lax.rsqrt(jnp.mean(x2 * x2, axis=-1, keepdims=True) + EPS) * g_ref[...]


def _combine_body(st_ref, x1_ref, slott_ref, ye_ref, o_ref, ybuf_ref, sem):
    _combine_kernel(st_ref, x1_ref, slott_ref, ye_ref, None, o_ref, ybuf_ref, sem)


def _combine(x1, slott, ye, starts, final_g=None):
    T = x1.shape[0]
    P = TOKEN_TILE
    in_specs = [pl.BlockSpec((P, D_MODEL), lambda i, st: (i, 0)),
                pl.BlockSpec((P, LANES), lambda i, st: (i, 0)),
                pl.BlockSpec(memory_space=pl.ANY)]
    args = [starts, x1, slott, ye]
    body = _combine_body
    if final_g is not None:
        in_specs.append(pl.BlockSpec((1, D_MODEL), lambda i, st: (0, 0)))
        args.append(final_g)
        body = _combine_kernel
    grid_spec = pltpu.PrefetchScalarGridSpec(
        num_scalar_prefetch=1,
        grid=(T // P,),
        in_specs=in_specs,
        out_specs=pl.BlockSpec((P, D_MODEL), lambda i, st: (i, 0)),
        scratch_shapes=[pltpu.VMEM((2, STACK_ROWS, D_MODEL), bf16),
                        pltpu.SemaphoreType.DMA((2, N_EXPERTS))],
    )
    return pl.pallas_call(
        body,
        grid_spec=grid_spec,
        out_shape=jax.ShapeDtypeStruct((T, D_MODEL), f32),
        compiler_params=_cparams(("arbitrary",)),
        name="combine",
    )(*args)


def _rope_tables(seq):
    half = ROPE_DIMS // 2
    pos = jnp.arange(seq, dtype=f32)
    inv_freq = jnp.exp(-jnp.log(jnp.float32(ROPE_THETA)) * jnp.arange(half, dtype=f32) * (2.0 / ROPE_DIMS))
    ang = pos[:, None] * inv_freq[None, :]
    cos, sin = jnp.cos(ang), jnp.sin(ang)
    rest = HEAD_DIM - ROPE_DIMS
    ones = jnp.ones((seq, rest), f32)
    zeros_h = jnp.zeros((seq, half), f32)
    zeros_r = jnp.zeros((seq, rest), f32)
    c = jnp.concatenate([cos, cos, ones], axis=1)
    s1 = jnp.concatenate([zeros_h, sin, zeros_r], axis=1)
    s2 = jnp.concatenate([-sin, zeros_h, zeros_r], axis=1)
    rep = LANES // HEAD_DIM
    return jnp.tile(c, (1, rep)), jnp.tile(s1, (1, rep)), jnp.tile(s2, (1, rep))


def _per_lane(v):
    return jnp.repeat(v.astype(f32), HEAD_DIM)[None, :]


def _trunk(x, params):
    batch, seq, _ = x.shape
    T = batch * seq
    cap = CAPACITY_FACTOR * T // N_EXPERTS
    assert seq % (RET_CHUNK * RET_CHUNKS_PER_STEP) == 0 and seq % PROJ_TILE == 0
    assert seq % (BLOCK * ATTN_QBLOCKS) == 0 and T % (TOKEN_TILE * min(RANK_TILES, T // TOKEN_TILE)) == 0
    assert cap % BF16_ROWS == 0 and cap % min(PROJ_TILE, cap) == 0
    x = x.reshape(T, D_MODEL)
    rc, rs1, rs2 = _rope_tables(seq)
    depth = params["w_in"].shape[0]
    for l in range(depth):
        q, kv, ret = _inproj(x, params["norm1_g"][l][None, :], params["w_in"], l, rc, rs1, rs2, seq)
        sink_rows = jnp.broadcast_to(params["attn_sink"][l].astype(f32)[:, None], (N_ATTN_HEADS, LANES))
        a_out = _attention(q, kv, sink_rows, params["attn_out_g"][l][None, :], batch, seq)
        r_out = _retention(ret, _per_lane(params["ret_decay_fwd"][l]), _per_lane(params["ret_decay_bwd"][l]),
                           params["ret_out_g"][l][None, :], batch, seq)
        x1, xaug, afft = _outproj(a_out, r_out, x, params["w_out"], params["norm2_g"][l][None, :],
                                  params["w_router"], l)
        slot, slott, starts = _select(afft, cap)
        xe = _dispatch(xaug, slot, starts, cap)
        ye = _ffn(xe, params["w_gate"], params["w_up"], params["w_down"], l, cap)
        final_g = params["final_norm_g"][None, :] if l == depth - 1 else None
        x = _combine(x1, slott, ye, starts, final_g)
    return x.reshape(batch, seq, D_MODEL)


def kernel(x_prompt, x_sample, norm1_g, w_in, attn_sink, ret_decay_fwd, ret_decay_bwd, attn_out_g, ret_out_g,
           w_out, norm2_g, w_router, w_gate, w_up, w_down, final_norm_g):
    wr = jnp.pad(jnp.swapaxes(w_router, 1, 2), ((0, 0), (0, LANES - N_EXPERTS), (0, 0)))
    params = dict(
        norm1_g=norm1_g, w_in=w_in.astype(bf16), attn_sink=attn_sink, ret_decay_fwd=ret_decay_fwd,
        ret_decay_bwd=ret_decay_bwd, attn_out_g=attn_out_g, ret_out_g=ret_out_g, w_out=w_out.astype(bf16),
        norm2_g=norm2_g, w_router=wr.astype(bf16), w_gate=w_gate.astype(bf16), w_up=w_up.astype(bf16),
        w_down=w_down.astype(bf16), final_norm_g=final_norm_g)
    return (_trunk(x_prompt, params), _trunk(x_sample, params))
```

```python
import functools

import jax
import jax.numpy as jnp
from jax import lax
from jax.experimental import pallas as pl
from jax.experimental.pallas import tpu as pltpu

f32 = jnp.float32
bf16 = jnp.bfloat16
i32 = jnp.int32

D_MODEL = 1024
HEAD_DIM = 64
N_ATTN_HEADS = 8
N_KV_HEADS = 2
N_RET_HEADS = 8
ATTN_WIDTH = N_ATTN_HEADS * HEAD_DIM
KV_WIDTH = N_KV_HEADS * HEAD_DIM
RET_WIDTH = N_RET_HEADS * HEAD_DIM
QKV_COLS = ATTN_WIDTH + 2 * KV_WIDTH
IN_COLS = QKV_COLS + 4 * RET_WIDTH
WINDOW = 128
BLOCK = 128
RET_CHUNK = 128
ROPE_THETA = 500000.0
ROPE_DIMS = HEAD_DIM // 4
N_EXPERTS = 16
CAPACITY_FACTOR = 2
D_FF = 2 * D_MODEL
EPS = 1e-6
NEG = -1e30

LANES = 128
BF16_ROWS = 16
GATE_COLS = LANES
XAUG_COLS = D_MODEL + GATE_COLS
TOKEN_TILE = 256
DISPATCH_CHUNK = 512
COMBINE_CHUNK = 1024
STACK_ROWS = N_EXPERTS * (TOKEN_TILE + 2 * BF16_ROWS)
STACK_ROWS = -(-STACK_ROWS // COMBINE_CHUNK) * COMBINE_CHUNK
PROJ_TILE = 512
RANK_TILES = 16
ATTN_QBLOCKS = 8
RET_CHUNKS_PER_STEP = 8
VMEM_LIMIT = 48 * 1024 * 1024


def _cparams(sem):
    return pltpu.CompilerParams(dimension_semantics=sem, vmem_limit_bytes=VMEM_LIMIT)


def _sigmoid(x):
    return 1.0 / (1.0 + jnp.exp(-x))


def _log_sigmoid(x):
    return jnp.minimum(x, 0.0) - jnp.log(1.0 + jnp.exp(-jnp.abs(x)))


def _inproj_kernel(x_ref, g_ref, w_ref, c_ref, s1_ref, s2_ref, q_ref, kv_ref, ret_ref):
    x = x_ref[...]
    h = x * lax.rsqrt(jnp.mean(x * x, axis=-1, keepdims=True) + EPS) * g_ref[...]
    hb = h.astype(bf16)
    c, s1, s2 = c_ref[...], s1_ref[...], s2_ref[...]

    def rope(p):
        half = ROPE_DIMS // 2
        return p * c + pltpu.roll(p, half, 1) * s1 + pltpu.roll(p, LANES - half, 1) * s2

    qkv = jnp.dot(hb, w_ref[0, :, 0:QKV_COLS], preferred_element_type=f32)
    for j in range(ATTN_WIDTH // LANES):
        q_ref[:, j * LANES:(j + 1) * LANES] = rope(qkv[:, j * LANES:(j + 1) * LANES]).astype(bf16)
    kv_ref[:, 0:KV_WIDTH] = rope(qkv[:, ATTN_WIDTH:ATTN_WIDTH + KV_WIDTH]).astype(bf16)
    kv_ref[:, KV_WIDTH:2 * KV_WIDTH] = qkv[:, ATTN_WIDTH + KV_WIDTH:QKV_COLS].astype(bf16)
    for j in range(4):
        lo = QKV_COLS + j * RET_WIDTH
        ret_ref[:, j * RET_WIDTH:(j + 1) * RET_WIDTH] = jnp.dot(
            hb, w_ref[0, :, lo:lo + RET_WIDTH], preferred_element_type=f32).astype(bf16)


def _inproj(x, g, w, layer, rc, rs1, rs2, seq):
    T = x.shape[0]
    tm = PROJ_TILE
    npos = seq // tm
    row = lambda i: (i, 0)
    pos = lambda i: (i % npos, 0)
    fixed = lambda i: (0, 0)
    return pl.pallas_call(
        _inproj_kernel,
        grid=(T // tm,),
        in_specs=[pl.BlockSpec((tm, D_MODEL), row), pl.BlockSpec((1, D_MODEL), fixed),
                  pl.BlockSpec((1, D_MODEL, IN_COLS), lambda i: (layer, 0, 0)),
                  pl.BlockSpec((tm, LANES), pos), pl.BlockSpec((tm, LANES), pos),
                  pl.BlockSpec((tm, LANES), pos)],
        out_specs=[pl.BlockSpec((tm, ATTN_WIDTH), row), pl.BlockSpec((tm, 2 * KV_WIDTH), row),
                   pl.BlockSpec((tm, 4 * RET_WIDTH), row)],
        out_shape=[jax.ShapeDtypeStruct((T, ATTN_WIDTH), bf16),
                   jax.ShapeDtypeStruct((T, 2 * KV_WIDTH), bf16),
                   jax.ShapeDtypeStruct((T, 4 * RET_WIDTH), bf16)],
        compiler_params=_cparams(("arbitrary",)),
        name="inproj",
    )(x, g, w, rc, rs1, rs2)


def _attn_kernel(q_ref, kp_ref, kc_ref, kn_ref, sink_ref, g_ref, o_ref):
    n = pl.program_id(1)
    nsteps = pl.num_programs(1)
    QB = ATTN_QBLOCKS
    kv = jnp.concatenate([kp_ref[...], kc_ref[...], kn_ref[...]], axis=0)
    rows = kv.shape[0]
    lane = lax.broadcasted_iota(i32, (rows, LANES), 1)
    low = lane < HEAD_DIM
    kf = kv[:, 0:LANES].astype(f32) * (HEAD_DIM ** -0.5)
    kr = pltpu.roll(kf, HEAD_DIM, 1)
    k_low = [jnp.where(low, kf, 0.0).astype(bf16), jnp.where(low, kr, 0.0).astype(bf16)]
    k_high = [jnp.where(low, 0.0, kr).astype(bf16), jnp.where(low, 0.0, kf).astype(bf16)]
    vt = kv[:, LANES:2 * LANES].astype(f32).T
    vs = jnp.concatenate([vt[HEAD_DIM:], vt[:HEAD_DIM]], axis=0)
    sub = lax.broadcasted_iota(i32, (LANES, rows), 0)
    top = sub < HEAD_DIM
    ones_mid = (sub == HEAD_DIM).astype(f32)
    ones_top = (sub == 0).astype(f32)
    vt_low = [jnp.where(top, vt, ones_mid).astype(bf16), jnp.where(top, vs, ones_mid).astype(bf16)]
    vt_high = [jnp.where(top, ones_top, vs).astype(bf16), jnp.where(top, ones_top, vt).astype(bf16)]

    ki = lax.broadcasted_iota(i32, (3 * BLOCK, BLOCK), 0)
    qi = lax.broadcasted_iota(i32, (3 * BLOCK, BLOCK), 1)
    in_window = jnp.abs(ki - BLOCK - qi) <= WINDOW
    topq = lax.broadcasted_iota(i32, (LANES, BLOCK), 0) < HEAD_DIM
    grp = N_ATTN_HEADS // N_KV_HEADS
    g_row = g_ref[...]

    def scores(i):
        blk = n * QB + i
        valid = in_window & ((ki >= BLOCK) | (blk > 0)) & ((ki < 2 * BLOCK) | (blk < nsteps * QB - 1))
        bias = jnp.where(valid, 0.0, NEG)
        keys = slice(i * BLOCK, (i + 3) * BLOCK)
        out = []
        for h in range(N_ATTN_HEADS):
            qp = q_ref[i * BLOCK:(i + 1) * BLOCK, (h // 2) * LANES:(h // 2 + 1) * LANES]
            kx = (k_high if h % 2 else k_low)[h // grp]
            s = lax.dot_general(kx[keys], qp, (((1,), (1,)), ((), ())), preferred_element_type=f32)
            out.append(jnp.concatenate([s[0:BLOCK] + bias[0:BLOCK], s[BLOCK:2 * BLOCK],
                                        s[2 * BLOCK:] + bias[2 * BLOCK:]], axis=0))
        return out

    pending = scores(0)
    for i in range(QB):
        current = pending
        if i + 1 < QB:
            pending = scores(i + 1)
        keys = slice(i * BLOCK, (i + 3) * BLOCK)
        probs, maxes = [], []
        for h in range(N_ATTN_HEADS):
            sk = sink_ref[h:h + 1, 0:1]
            m = jnp.maximum(jnp.max(current[h], axis=0, keepdims=True), sk)
            probs.append(jnp.exp(current[h] - m).astype(bf16))
            maxes.append(m)
        halves = []
        for h in range(N_ATTN_HEADS):
            vx = (vt_high if h % 2 else vt_low)[h // grp]
            ones_at = 0 if h % 2 else HEAD_DIM
            oa = jnp.dot(vx[:, keys], probs[h], preferred_element_type=f32)
            denom = oa[ones_at:ones_at + 1, :] + jnp.exp(sink_ref[h:h + 1, 0:1] - maxes[h])
            halves.append(oa / denom)
        pieces = [jnp.where(topq, halves[2 * j], halves[2 * j + 1]).T for j in range(N_ATTN_HEADS // 2)]
        a = jnp.concatenate(pieces, axis=1)
        a = a * lax.rsqrt(jnp.mean(a * a, axis=-1, keepdims=True) + EPS) * g_row
        o_ref[i * BLOCK:(i + 1) * BLOCK, :] = a.astype(bf16)


def _attention(q, kv, sink_rows, g, batch, seq):
    T = q.shape[0]
    QB = ATTN_QBLOCKS
    nb = seq // BLOCK
    ns = nb // QB
    cur = lambda b, n: (b * ns + n, 0)
    prev = lambda b, n: (b * nb + jnp.maximum(n * QB - 1, 0), 0)
    nxt = lambda b, n: (b * nb + jnp.minimum(n * QB + QB, nb - 1), 0)
    fixed = lambda b, n: (0, 0)
    return pl.pallas_call(
        _attn_kernel,
        grid=(batch, ns),
        in_specs=[pl.BlockSpec((QB * BLOCK, ATTN_WIDTH), cur),
                  pl.BlockSpec((BLOCK, 2 * KV_WIDTH), prev),
                  pl.BlockSpec((QB * BLOCK, 2 * KV_WIDTH), cur),
                  pl.BlockSpec((BLOCK, 2 * KV_WIDTH), nxt),
                  pl.BlockSpec((N_ATTN_HEADS, LANES), fixed),
                  pl.BlockSpec((1, ATTN_WIDTH), fixed)],
        out_specs=pl.BlockSpec((QB * BLOCK, ATTN_WIDTH), cur),
        out_shape=jax.ShapeDtypeStruct((T, ATTN_WIDTH), bf16),
        compiler_params=_cparams(("arbitrary", "arbitrary")),
        name="attention",
    )(q, kv, kv, kv, sink_rows, g)


N_PAIRS = RET_WIDTH // LANES


def _pair_masks():
    r = lax.broadcasted_iota(i32, (LANES, LANES), 0)
    c = lax.broadcasted_iota(i32, (LANES, LANES), 1)
    low = c < HEAD_DIM
    blockdiag = (r < HEAD_DIM) == (c < HEAD_DIM)
    return low, blockdiag


def _ret_bstate_kernel(k_ref, v_ref, db_ref, sb_ref, st_ref):
    @pl.when(pl.program_id(1) == 0)
    def _():
        st_ref[...] = jnp.zeros(st_ref.shape, f32)

    C = RET_CHUNK
    lg = _log_sigmoid(db_ref[...])
    idx = lax.broadcasted_iota(i32, (C, RET_WIDTH), 0).astype(f32)
    zeta = jnp.exp(lg * idx)
    gc = jnp.exp(lg * float(C))
    _, blockdiag = _pair_masks()
    for c in reversed(range(RET_CHUNKS_PER_STEP)):
        rows = slice(c * C, (c + 1) * C)
        for j in range(N_PAIRS):
            lanes = slice(j * LANES, (j + 1) * LANES)
            kp = k_ref[rows, lanes].astype(f32) * (HEAD_DIM ** -0.5)
            vp = v_ref[rows, lanes]
            st = st_ref[j]
            sb_ref[c, j] = st.astype(bf16)
            kz = (kp * zeta[:, lanes]).astype(bf16)
            upd = lax.dot_general(kz, vp, (((0,), (0,)), ((), ())), preferred_element_type=f32)
            st_ref[j] = jnp.where(blockdiag, gc[:, lanes] * st + upd, 0.0)


def _ret_main_kernel(q_ref, k_ref, v_ref, gate_ref, sb_ref, df_ref, db_ref, gn_ref, o_ref, st_ref):
    @pl.when(pl.program_id(1) == 0)
    def _():
        st_ref[...] = jnp.zeros(st_ref.shape, f32)

    C = RET_CHUNK
    lgf = _log_sigmoid(df_ref[...])
    lgb = _log_sigmoid(db_ref[...])
    idx = lax.broadcasted_iota(i32, (C, RET_WIDTH), 0).astype(f32)
    xi_f = jnp.exp(lgf * (idx + 1.0))
    xi_b = jnp.exp(lgb * (float(C) - idx))
    zeta_f = jnp.exp(lgf * (float(C - 1) - idx))
    gc_f = jnp.exp(lgf * float(C))
    low, blockdiag = _pair_masks()
    diff = (lax.broadcasted_iota(i32, (C, C), 0) - lax.broadcasted_iota(i32, (C, C), 1)).astype(f32)

    def head_decay(h):
        a = lgf[:, h * HEAD_DIM:h * HEAD_DIM + 1]
        b = lgb[:, h * HEAD_DIM:h * HEAD_DIM + 1]
        return jnp.where(diff >= 0.0, jnp.exp(a * jnp.maximum(diff, 0.0)),
                         jnp.exp(b * jnp.maximum(-diff, 0.0)))

    decay = [jnp.concatenate([head_decay(2 * j), head_decay(2 * j + 1)], axis=0) for j in range(N_PAIRS)]
    gn = gn_ref[...]
    inv_hd = 1.0 / HEAD_DIM

    for c in range(RET_CHUNKS_PER_STEP):
        rows = slice(c * C, (c + 1) * C)
        for j in range(N_PAIRS):
            lanes = slice(j * LANES, (j + 1) * LANES)
            qp = q_ref[rows, lanes]
            qf = qp.astype(f32)
            kp = k_ref[rows, lanes].astype(f32) * (HEAD_DIM ** -0.5)
            kb = kp.astype(bf16)
            vp = v_ref[rows, lanes]
            zero = jnp.zeros_like(qp)
            q2 = jnp.concatenate([jnp.where(low, qp, zero), jnp.where(low, zero, qp)], axis=0)
            s2 = lax.dot_general(q2, kb, (((1,), (1,)), ((), ())), preferred_element_type=f32)
            o2 = jnp.dot((s2 * decay[j]).astype(bf16), vp, preferred_element_type=f32)
            inner = jnp.where(low, o2[0:C], o2[C:2 * C])
            st = st_ref[j]
            cross_f = jnp.dot((qf * xi_f[:, lanes]).astype(bf16), st.astype(bf16),
                              preferred_element_type=f32)
            cross_b = jnp.dot((qf * xi_b[:, lanes]).astype(bf16), sb_ref[c, j],
                              preferred_element_type=f32)
            o = inner + cross_f + cross_b
            s_lo = jnp.sum(jnp.where(low, o, 0.0), axis=-1, keepdims=True)
            s_hi = jnp.sum(jnp.where(low, 0.0, o), axis=-1, keepdims=True)
            d = o - jnp.where(low, s_lo, s_hi) * inv_hd
            dd = d * d
            v_lo = jnp.sum(jnp.where(low, dd, 0.0), axis=-1, keepdims=True)
            v_hi = jnp.sum(jnp.where(low, 0.0, dd), axis=-1, keepdims=True)
            var = jnp.where(low, v_lo, v_hi) * inv_hd
            on = d * lax.rsqrt(var + EPS) * gn[:, lanes]
            gate = gate_ref[rows, lanes].astype(f32)
            o_ref[rows, lanes] = (gate * _sigmoid(gate) * on).astype(bf16)
            kz = (kp * zeta_f[:, lanes]).T.astype(bf16)
            upd = jnp.dot(kz, vp, preferred_element_type=f32)
            st_ref[j] = jnp.where(blockdiag, gc_f[:, lanes] * st + upd, 0.0)


def _retention(ret, dec_f, dec_b, gn, batch, seq):
    T = ret.shape[0]
    C = RET_CHUNK
    ch = RET_CHUNKS_PER_STEP
    nblk = seq // (C * ch)
    rows = ch * C
    fixed = lambda b, i: (0, 0)
    col = lambda j: (lambda b, i: (b * nblk + i, j))
    rcol = lambda j: (lambda b, i: (b * nblk + nblk - 1 - i, j))
    sb = pl.pallas_call(
        _ret_bstate_kernel,
        grid=(batch, nblk),
        in_specs=[pl.BlockSpec((rows, RET_WIDTH), rcol(1)), pl.BlockSpec((rows, RET_WIDTH), rcol(2)),
                  pl.BlockSpec((1, RET_WIDTH), fixed)],
        out_specs=pl.BlockSpec((ch, N_PAIRS, LANES, LANES), lambda b, i: (b * nblk + nblk - 1 - i, 0, 0, 0)),
        out_shape=jax.ShapeDtypeStruct((T // C, N_PAIRS, LANES, LANES), bf16),
        scratch_shapes=[pltpu.VMEM((N_PAIRS, LANES, LANES), f32)],
        compiler_params=_cparams(("arbitrary", "arbitrary")),
        name="ret_bstate",
    )(ret, ret, dec_b)
    return pl.pallas_call(
        _ret_main_kernel,
        grid=(batch, nblk),
        in_specs=[pl.BlockSpec((rows, RET_WIDTH), col(0)), pl.BlockSpec((rows, RET_WIDTH), col(1)),
                  pl.BlockSpec((rows, RET_WIDTH), col(2)), pl.BlockSpec((rows, RET_WIDTH), col(3)),
                  pl.BlockSpec((ch, N_PAIRS, LANES, LANES), lambda b, i: (b * nblk + i, 0, 0, 0)),
                  pl.BlockSpec((1, RET_WIDTH), fixed), pl.BlockSpec((1, RET_WIDTH), fixed),
                  pl.BlockSpec((1, RET_WIDTH), fixed)],
        out_specs=pl.BlockSpec((rows, RET_WIDTH), col(0)),
        out_shape=jax.ShapeDtypeStruct((T, RET_WIDTH), bf16),
        scratch_shapes=[pltpu.VMEM((N_PAIRS, LANES, LANES), f32)],
        compiler_params=_cparams(("arbitrary", "arbitrary")),
        name="ret_main",
    )(ret, ret, ret, ret, sb, dec_f, dec_b, gn)


def _outproj_kernel(a_ref, r_ref, x_ref, w_ref, g_ref, wrt_ref, x1_ref, xaug_ref, afft_ref):
    tm = x_ref.shape[0]
    parts = [slice(0, tm // 2), slice(tm // 2, tm)]
    w = w_ref[0]
    proj = [jnp.dot(jnp.concatenate([a_ref[s, :], r_ref[s, :]], axis=1), w, preferred_element_type=f32)
            for s in parts]
    hbs = []
    for s, y in zip(parts, proj):
        x1 = x_ref[s, :] + y
        x1_ref[s, :] = x1
        h2 = x1 * lax.rsqrt(jnp.mean(x1 * x1, axis=-1, keepdims=True) + EPS) * g_ref[...]
        hb = h2.astype(bf16)
        xaug_ref[s, 0:D_MODEL] = hb
        hbs.append(hb)
    logits = [lax.dot_general(wrt_ref[0], hb, (((1,), (1,)), ((), ())), preferred_element_type=f32)
              for hb in hbs]
    for s, lt in zip(parts, logits):
        lg = lt[0:N_EXPERTS, :]
        e = jnp.exp(lg - jnp.max(lg, axis=0, keepdims=True))
        aff = e / jnp.sum(e, axis=0, keepdims=True)
        afft_ref[:, s] = aff
        hi = aff.astype(bf16).astype(f32)
        r1 = aff - hi
        mid = r1.astype(bf16).astype(f32)
        lo = (r1 - mid).astype(bf16).astype(f32)
        rows = jnp.concatenate([hi, mid, lo, jnp.zeros((LANES - 3 * N_EXPERTS, aff.shape[1]), f32)], axis=0)
        xaug_ref[s, D_MODEL:XAUG_COLS] = rows.T.astype(bf16)


def _outproj(a, r, x, w, g, wr, layer):
    T = x.shape[0]
    tm = PROJ_TILE
    row = lambda i: (i, 0)
    fixed = lambda i: (0, 0)
    lay = lambda i: (layer, 0, 0)
    return pl.pallas_call(
        _outproj_kernel,
        grid=(T // tm,),
        in_specs=[pl.BlockSpec((tm, ATTN_WIDTH), row), pl.BlockSpec((tm, RET_WIDTH), row),
                  pl.BlockSpec((tm, D_MODEL), row), pl.BlockSpec((1, D_MODEL, D_MODEL), lay),
                  pl.BlockSpec((1, D_MODEL), fixed), pl.BlockSpec((1, LANES, D_MODEL), lay)],
        out_specs=[pl.BlockSpec((tm, D_MODEL), row), pl.BlockSpec((tm, XAUG_COLS), row),
                   pl.BlockSpec((N_EXPERTS, tm), lambda i: (0, i))],
        out_shape=[jax.ShapeDtypeStruct((T, D_MODEL), f32), jax.ShapeDtypeStruct((T, XAUG_COLS), bf16),
                   jax.ShapeDtypeStruct((N_EXPERTS, T), f32)],
        compiler_params=_cparams(("arbitrary",)),
        name="outproj_router",
    )(a, r, x, w, g, wr)


SCAN_LANES = 512


def _thresh_kernel(a_ref, thr_ref, need_ref, *, cap):
    T = a_ref.shape[1]
    nchunk = T // SCAN_LANES

    def count_ge(cand):
        def body(c, acc):
            off = pl.multiple_of(c * SCAN_LANES, SCAN_LANES)
            keys = lax.bitcast_convert_type(a_ref[:, pl.ds(off, SCAN_LANES)], i32)
            return acc + (keys >= cand).astype(i32)
        acc = lax.fori_loop(0, nchunk, body, jnp.zeros((N_EXPERTS, SCAN_LANES), i32))
        return jnp.sum(acc, axis=1, keepdims=True)

    def bit_body(i, prefix):
        cand = prefix | (jnp.int32(1) << (30 - i))
        return jnp.where(count_ge(cand) >= cap, cand, prefix)

    thr = lax.fori_loop(0, 31, bit_body, jnp.zeros((N_EXPERTS, 1), i32))
    need = cap - count_ge(thr + 1)
    thr_ref[...] = jnp.broadcast_to(thr, thr_ref.shape)
    need_ref[...] = jnp.broadcast_to(need, need_ref.shape)


def _slots_kernel(a_ref, thr_ref, need_ref, slot_ref, slott_ref, start_ref, ceq_ref, csel_ref):
    @pl.when(pl.program_id(0) == 0)
    def _():
        ceq_ref[...] = jnp.zeros(ceq_ref.shape, f32)
        csel_ref[...] = jnp.zeros(csel_ref.shape, f32)

    P = TOKEN_TILE
    thr = thr_ref[:, 0:1]
    need = need_ref[:, 0:1].astype(f32)
    before = (lax.broadcasted_iota(i32, (P, P), 0) < lax.broadcasted_iota(i32, (P, P), 1)).astype(bf16)
    ceq = ceq_ref[:, 0:1]
    csel = csel_ref[:, 0:1]
    for j in range(a_ref.shape[1] // P):
        cols = slice(j * P, (j + 1) * P)
        keys = lax.bitcast_convert_type(a_ref[:, cols], i32)
        gt = keys > thr
        eq = keys == thr
        eqf = eq.astype(f32)
        eq_rank = jnp.dot(eqf.astype(bf16), before, preferred_element_type=f32) + ceq
        sel = gt | (eq & (eq_rank < need))
        self_ = sel.astype(f32)
        rank = jnp.dot(self_.astype(bf16), before, preferred_element_type=f32) + csel
        slot = jnp.where(sel, rank, -1.0)
        slot_ref[:, cols] = slot.astype(i32)
        padded = jnp.concatenate([slot, jnp.full((LANES - N_EXPERTS, P), -1.0, f32)], axis=0)
        slott_ref[cols, :] = padded.T.astype(i32)
        start_ref[j] = jnp.broadcast_to(csel, (N_EXPERTS, LANES)).astype(i32)
        ceq = ceq + jnp.sum(eqf, axis=1, keepdims=True)
        csel = csel + jnp.sum(self_, axis=1, keepdims=True)
    ceq_ref[...] = jnp.broadcast_to(ceq, ceq_ref.shape)
    csel_ref[...] = jnp.broadcast_to(csel, csel_ref.shape)


def _select(afft, cap):
    T = afft.shape[1]
    P = TOKEN_TILE
    nt = T // P
    rt = min(RANK_TILES, nt)
    thr, need = pl.pallas_call(
        functools.partial(_thresh_kernel, cap=cap),
        out_shape=[jax.ShapeDtypeStruct((N_EXPERTS, LANES), i32)] * 2,
        compiler_params=pltpu.CompilerParams(vmem_limit_bytes=VMEM_LIMIT),
        name="topk_threshold",
    )(afft)
    fixed = lambda i: (0, 0)
    slot, slott, starts = pl.pallas_call(
        _slots_kernel,
        grid=(nt // rt,),
        in_specs=[pl.BlockSpec((N_EXPERTS, rt * P), lambda i: (0, i)),
                  pl.BlockSpec((N_EXPERTS, LANES), fixed), pl.BlockSpec((N_EXPERTS, LANES), fixed)],
        out_specs=[pl.BlockSpec((N_EXPERTS, rt * P), lambda i: (0, i)),
                   pl.BlockSpec((rt * P, LANES), lambda i: (i, 0)),
                   pl.BlockSpec((rt, N_EXPERTS, LANES), lambda i: (i, 0, 0))],
        out_shape=[jax.ShapeDtypeStruct((N_EXPERTS, T), i32), jax.ShapeDtypeStruct((T, LANES), i32),
                   jax.ShapeDtypeStruct((nt, N_EXPERTS, LANES), i32)],
        scratch_shapes=[pltpu.VMEM((N_EXPERTS, LANES), f32), pltpu.VMEM((N_EXPERTS, LANES), f32)],
        compiler_params=_cparams(("arbitrary",)),
        name="slot_ranks",
    )(afft, thr, need)
    starts = jnp.concatenate([starts[:, :, 0], jnp.full((1, N_EXPERTS), cap, i32)], axis=0)
    return slot, slott, starts.reshape(-1)


BIG = 1 << 30


ROW_SHIFT = BF16_ROWS.bit_length() - 1


def _floor_tile(v):
    return (v >> ROW_SHIFT) << ROW_SHIFT


def _ceil_div_pow2(v, d):
    return (v + d - 1) >> (d.bit_length() - 1)


def _expert_vector(vals, shape, axis, fill=0):
    idx = lax.broadcasted_iota(i32, shape, axis)
    out = jnp.full(shape, fill, i32)
    for e, v in enumerate(vals):
        out = jnp.where(idx == e, v, out)
    return out


def _prefix(lengths):
    offs = [jnp.int32(0)]
    for n in lengths:
        offs.append(offs[-1] + n)
    return offs


def _split_bf16(tgt):
    hi = jnp.where(tgt >= 0, (tgt >> 6) << 6, 0)
    lo = jnp.where(tgt >= 0, tgt & 63, -1)
    return hi.astype(f32).astype(bf16), lo.astype(f32).astype(bf16)


def _dispatch_kernel(st_ref, x_ref, slot_ref, xe_ref, res_ref, stage_ref, sem):
    p = pl.program_id(0)
    nt = pl.num_programs(0)
    par = p % 2
    P = TOKEN_TILE
    R = DISPATCH_CHUNK
    E = N_EXPERTS

    @pl.when(p == 0)
    def _():
        stage_ref[...] = jnp.zeros(stage_ref.shape, bf16)
        res_ref[...] = jnp.zeros(res_ref.shape, bf16)

    start = [st_ref[p * E + e] for e in range(E)]
    stop = [st_ref[(p + 1) * E + e] for e in range(E)]
    base = [_floor_tile(s) for s in start]
    done = [_floor_tile(t) - b for t, b in zip(stop, base)]
    partial = [t - _floor_tile(t) for t in stop]
    length = [d + jnp.where(r > 0, BF16_ROWS, 0) for d, r in zip(done, partial)]
    offs = _prefix(length)
    total = offs[E]

    def out_copy(buf, e, src, dst, n):
        n = pl.multiple_of(n, BF16_ROWS)
        return pltpu.make_async_copy(res_ref.at[buf, pl.ds(pl.multiple_of(src, BF16_ROWS), n), :],
                                     xe_ref.at[e, pl.ds(pl.multiple_of(dst, BF16_ROWS), n), :],
                                     sem.at[buf, e])

    slot = slot_ref[...]
    shift = _expert_vector([offs[e] - base[e] for e in range(E)], (E, P), 0)
    tgt_hi, tgt_lo = _split_bf16(jnp.where(slot >= 0, slot + shift, -1))
    pad = jnp.zeros((LANES - E, P), bf16)
    tgt_parts = jnp.concatenate([tgt_hi, pad, tgt_lo, pad], axis=0)
    first = _expert_vector(offs[:E], (1, LANES), 1, fill=BIG)
    last = _expert_vector(offs[1:], (1, LANES), 1, fill=BIG)
    x = x_ref[...]

    def chunk(k, carry):
        row0 = pl.multiple_of(k * R, R)
        rid = row0 + lax.broadcasted_iota(i32, (R, LANES), 0)
        owner = ((rid >= first) & (rid < last)).astype(bf16)
        want = jnp.dot(jnp.concatenate([owner, owner], axis=1), tgt_parts,
                       preferred_element_type=f32)
        rid2 = (row0 + lax.broadcasted_iota(i32, (R, P), 0)).astype(f32)
        onehot = (want == rid2).astype(bf16)
        res_ref[par, pl.ds(row0, R), :] = jnp.dot(onehot, x, preferred_element_type=f32).astype(bf16)
        return carry

    lax.fori_loop(0, _ceil_div_pow2(total, R), chunk, 0)

    for e in range(E):
        head = pl.multiple_of(offs[e], BF16_ROWS)
        tail = pl.multiple_of(head + done[e], BF16_ROWS)
        srows = slice(e * BF16_ROWS, (e + 1) * BF16_ROWS)
        res_ref[par, pl.ds(head, BF16_ROWS), :] = (res_ref[par, pl.ds(head, BF16_ROWS), :]
                                                   + stage_ref[srows, :])
        stage_ref[srows, :] = jnp.where(partial[e] > 0, res_ref[par, pl.ds(tail, BF16_ROWS), :],
                                        jnp.zeros((BF16_ROWS, XAUG_COLS), bf16))

        @pl.when(done[e] > 0)
        def _(e=e, head=head):
            out_copy(par, e, head, base[e], done[e]).start()

    @pl.when(p > 0)
    def _():
        for e in range(E):
            n_prev = base[e] - _floor_tile(st_ref[(p - 1) * E + e])

            @pl.when(n_prev > 0)
            def _(e=e, n_prev=n_prev):
                out_copy(1 - par, e, 0, 0, n_prev).wait()

    @pl.when(p == nt - 1)
    def _():
        for e in range(E):
            @pl.when(done[e] > 0)
            def _(e=e):
                out_copy(par, e, 0, 0, done[e]).wait()


def _dispatch(xaug, slot, starts, cap):
    T = xaug.shape[0]
    P = TOKEN_TILE
    grid_spec = pltpu.PrefetchScalarGridSpec(
        num_scalar_prefetch=1,
        grid=(T // P,),
        in_specs=[pl.BlockSpec((P, XAUG_COLS), lambda i, st: (i, 0)),
                  pl.BlockSpec((N_EXPERTS, P), lambda i, st: (0, i))],
        out_specs=pl.BlockSpec(memory_space=pl.ANY),
        scratch_shapes=[pltpu.VMEM((2, STACK_ROWS, XAUG_COLS), bf16),
                        pltpu.VMEM((N_EXPERTS * BF16_ROWS, XAUG_COLS), bf16),
                        pltpu.SemaphoreType.DMA((2, N_EXPERTS))],
    )
    return pl.pallas_call(
        _dispatch_kernel,
        grid_spec=grid_spec,
        out_shape=jax.ShapeDtypeStruct((N_EXPERTS, cap, XAUG_COLS), bf16),
        compiler_params=_cparams(("arbitrary",)),
        name="dispatch",
    )(starts, xaug, slot)


FF_CHUNK = 512


def _ffn_kernel(xe_ref, wg_ref, wu_ref, wd_ref, ye_ref):
    e = pl.program_id(0)
    x = xe_ref[0, :, 0:D_MODEL]
    gsplit = xe_ref[0, :, D_MODEL:XAUG_COLS].astype(f32)
    lane = lax.broadcasted_iota(i32, gsplit.shape, 1)
    mine = (lane == e) | (lane == e + N_EXPERTS) | (lane == e + 2 * N_EXPERTS)
    gate = jnp.sum(jnp.where(mine, gsplit, 0.0), axis=-1, keepdims=True)
    acc = jnp.zeros((x.shape[0], D_MODEL), f32)
    for f in range(0, D_FF, FF_CHUNK):
        g = jnp.dot(x, wg_ref[0, 0, :, f:f + FF_CHUNK], preferred_element_type=f32)
        u = jnp.dot(x, wu_ref[0, 0, :, f:f + FF_CHUNK], preferred_element_type=f32)
        h = (g * _sigmoid(g) * u).astype(bf16)
        acc = acc + jnp.dot(h, wd_ref[0, 0, f:f + FF_CHUNK, :], preferred_element_type=f32)
    ye_ref[0] = (acc * gate).astype(bf16)


def _ffn(xe, wg, wu, wd, layer, cap):
    tm = min(PROJ_TILE, cap)
    wspec = lambda rows, cols: pl.BlockSpec((1, 1, rows, cols), lambda e, j: (layer, e, 0, 0))
    return pl.pallas_call(
        _ffn_kernel,
        grid=(N_EXPERTS, cap // tm),
        in_specs=[pl.BlockSpec((1, tm, XAUG_COLS), lambda e, j: (e, j, 0)),
                  wspec(D_MODEL, D_FF), wspec(D_MODEL, D_FF), wspec(D_FF, D_MODEL)],
        out_specs=pl.BlockSpec((1, tm, D_MODEL), lambda e, j: (e, j, 0)),
        out_shape=jax.ShapeDtypeStruct((N_EXPERTS, cap, D_MODEL), bf16),
        compiler_params=_cparams(("arbitrary", "arbitrary")),
        name="experts",
    )(xe, wg, wu, wd)


def _combine_kernel(st_ref, x1_ref, slott_ref, ye_ref, g_ref, o_ref, ybuf_ref, sem):
    p = pl.program_id(0)
    nt = pl.num_programs(0)
    par = p % 2
    P = TOKEN_TILE
    KC = COMBINE_CHUNK
    E = N_EXPERTS

    def geometry(q):
        start = [st_ref[q * E + e] for e in range(E)]
        stop = [st_ref[(q + 1) * E + e] for e in range(E)]
        base = [_floor_tile(s) for s in start]
        length = [jnp.where(t > s, _floor_tile(t + BF16_ROWS - 1) - b, 0)
                  for s, t, b in zip(start, stop, base)]
        return base, length, _prefix(length)

    def in_copy(buf, e, src, dst, n):
        n = pl.multiple_of(n, BF16_ROWS)
        return pltpu.make_async_copy(ye_ref.at[e, pl.ds(pl.multiple_of(src, BF16_ROWS), n), :],
                                     ybuf_ref.at[buf, pl.ds(pl.multiple_of(dst, BF16_ROWS), n), :],
                                     sem.at[buf, e])

    def fetch(q, buf):
        base, length, offs = geometry(q)
        for e in range(E):
            @pl.when(length[e] > 0)
            def _(e=e):
                in_copy(buf, e, base[e], offs[e], length[e]).start()

    @pl.when(p == 0)
    def _():
        ybuf_ref[...] = jnp.zeros(ybuf_ref.shape, bf16)
        fetch(0, 0)

    @pl.when(p + 1 < nt)
    def _():
        fetch(p + 1, 1 - par)

    base, length, offs = geometry(p)
    total = offs[E]
    slots = slott_ref[...]
    shift = _expert_vector([offs[e] - base[e] for e in range(E)], (1, LANES), 1)
    tgt_hi, tgt_lo = _split_bf16(jnp.where(slots >= 0, slots + shift, -1))
    tgt_parts = jnp.concatenate([tgt_hi, tgt_lo], axis=1)
    first = _expert_vector(offs[:E], (E, KC), 0)
    last = _expert_vector(offs[1:], (E, KC), 0)
    pad = jnp.zeros((LANES - E, KC), bf16)
    for e in range(E):
        @pl.when(length[e] > 0)
        def _(e=e):
            in_copy(par, e, 0, 0, length[e]).wait()

    o_ref[...] = x1_ref[...]

    def chunk(k, carry):
        col0 = pl.multiple_of(k * KC, KC)
        cid = col0 + lax.broadcasted_iota(i32, (E, KC), 1)
        own = ((cid >= first) & (cid < last)).astype(bf16)
        owner = jnp.concatenate([own, pad, own, pad], axis=0)
        want = jnp.dot(tgt_parts, owner, preferred_element_type=f32)
        cid2 = (col0 + lax.broadcasted_iota(i32, (P, KC), 1)).astype(f32)
        onehot = (want == cid2).astype(bf16)
        o_ref[...] = o_ref[...] + jnp.dot(onehot, ybuf_ref[par, pl.ds(col0, KC), :],
                                          preferred_element_type=f32)
        return carry

    lax.fori_loop(0, _ceil_div_pow2(total, KC), chunk, 0)
    if g_ref is not None:
        x2 = o_ref[...]
        o_ref[...] = x2 * lax.rsqrt(jnp.mean(x2 * x2, axis=-1, keepdims=True) + EPS) * g_ref[...]


def _combine_body(st_ref, x1_ref, slott_ref, ye_ref, o_ref, ybuf_ref, sem):
    _combine_kernel(st_ref, x1_ref, slott_ref, ye_ref, None, o_ref, ybuf_ref, sem)


def _combine(x1, slott, ye, starts, final_g=None):
    T = x1.shape[0]
    P = TOKEN_TILE
    in_specs = [pl.BlockSpec((P, D_MODEL), lambda i, st: (i, 0)),
                pl.BlockSpec((P, LANES), lambda i, st: (i, 0)),
                pl.BlockSpec(memory_space=pl.ANY)]
    args = [starts, x1, slott, ye]
    body = _combine_body
    if final_g is not None:
        in_specs.append(pl.BlockSpec((1, D_MODEL), lambda i, st: (0, 0)))
        args.append(final_g)
        body = _combine_kernel
    grid_spec = pltpu.PrefetchScalarGridSpec(
        num_scalar_prefetch=1,
        grid=(T // P,),
        in_specs=in_specs,
        out_specs=pl.BlockSpec((P, D_MODEL), lambda i, st: (i, 0)),
        scratch_shapes=[pltpu.VMEM((2, STACK_ROWS, D_MODEL), bf16),
                        pltpu.SemaphoreType.DMA((2, N_EXPERTS))],
    )
    return pl.pallas_call(
        body,
        grid_spec=grid_spec,
        out_shape=jax.ShapeDtypeStruct((T, D_MODEL), f32),
        compiler_params=_cparams(("arbitrary",)),
        name="combine",
    )(*args)


def _rope_tables(seq):
    half = ROPE_DIMS // 2
    pos = jnp.arange(seq, dtype=f32)
    inv_freq = jnp.exp(-jnp.log(jnp.float32(ROPE_THETA)) * jnp.arange(half, dtype=f32) * (2.0 / ROPE_DIMS))
    ang = pos[:, None] * inv_freq[None, :]
    cos, sin = jnp.cos(ang), jnp.sin(ang)
    rest = HEAD_DIM - ROPE_DIMS
    ones = jnp.ones((seq, rest), f32)
    zeros_h = jnp.zeros((seq, half), f32)
    zeros_r = jnp.zeros((seq, rest), f32)
    c = jnp.concatenate([cos, cos, ones], axis=1)
    s1 = jnp.concatenate([zeros_h, sin, zeros_r], axis=1)
    s2 = jnp.concatenate([-sin, zeros_h, zeros_r], axis=1)
    rep = LANES // HEAD_DIM
    return jnp.tile(c, (1, rep)), jnp.tile(s1, (1, rep)), jnp.tile(s2, (1, rep))


def _per_lane(v):
    return jnp.repeat(v.astype(f32), HEAD_DIM)[None, :]


def _trunk(x, params):
    batch, seq, _ = x.shape
    T = batch * seq
    cap = CAPACITY_FACTOR * T // N_EXPERTS
    assert seq % (RET_CHUNK * RET_CHUNKS_PER_STEP) == 0 and seq % PROJ_TILE == 0
    assert seq % (BLOCK * ATTN_QBLOCKS) == 0 and T % (TOKEN_TILE * min(RANK_TILES, T // TOKEN_TILE)) == 0
    assert cap % BF16_ROWS == 0 and cap % min(PROJ_TILE, cap) == 0
    x = x.reshape(T, D_MODEL)
    rc, rs1, rs2 = _rope_tables(seq)
    depth = params["w_in"].shape[0]
    for l in range(depth):
        q, kv, ret = _inproj(x, params["norm1_g"][l][None, :], params["w_in"], l, rc, rs1, rs2, seq)
        sink_rows = jnp.broadcast_to(params["attn_sink"][l].astype(f32)[:, None], (N_ATTN_HEADS, LANES))
        a_out = _attention(q, kv, sink_rows, params["attn_out_g"][l][None, :], batch, seq)
        r_out = _retention(ret, _per_lane(params["ret_decay_fwd"][l]), _per_lane(params["ret_decay_bwd"][l]),
                           params["ret_out_g"][l][None, :], batch, seq)
        x1, xaug, afft = _outproj(a_out, r_out, x, params["w_out"], params["norm2_g"][l][None, :],
                                  params["w_router"], l)
        slot, slott, starts = _select(afft, cap)
        xe = _dispatch(xaug, slot, starts, cap)
        ye = _ffn(xe, params["w_gate"], params["w_up"], params["w_down"], l, cap)
        final_g = params["final_norm_g"][None, :] if l == depth - 1 else None
        x = _combine(x1, slott, ye, starts, final_g)
    return x.reshape(batch, seq, D_MODEL)


def kernel(x_prompt, x_sample, norm1_g, w_in, attn_sink, ret_decay_fwd, ret_decay_bwd, attn_out_g, ret_out_g,
           w_out, norm2_g, w_router, w_gate, w_up, w_down, final_norm_g):
    wr = jnp.pad(jnp.swapaxes(w_router, 1, 2), ((0, 0), (0, LANES - N_EXPERTS), (0, 0)))
    params = dict(
        norm1_g=norm1_g, w_in=w_in.astype(bf16), attn_sink=attn_sink, ret_decay_fwd=ret_decay_fwd,
        ret_decay_bwd=ret_decay_bwd, attn_out_g=attn_out_g, ret_out_g=ret_out_g, w_out=w_out.astype(bf16),
        norm2_g=norm2_g, w_router=wr.astype(bf16), w_gate=w_gate.astype(bf16), w_up=w_up.astype(bf16),
        w_down=w_down.astype(bf16), final_norm_g=final_norm_g)
    return (_trunk(x_prompt, params), _trunk(x_sample, params))
```
